```python
import jax, jax.numpy as jnp
from jax import lax
import numpy as np

D_MODEL = 1024
BATCH = 8
SEQ = 2048
DEPTH = 1
DEC_BATCH = 128
DEC_SEQ = 1
PAST_LEN = 16384
PAGE_SIZE = 128

CONV_DIM = D_MODEL // 2
CONV_W = 3
SGU_DIM = D_MODEL - CONV_DIM
SGU_GROUPS = 4
SGU_GROUP_DIM = SGU_DIM // SGU_GROUPS
CHUNK = 128
IN_DIM = 3 * CONV_DIM + 2 * SGU_DIM
PEER_HEADS = 8
N_KEYS = 128
N_EXPERTS = N_KEYS * N_KEYS
PEER_TOPK = 16
D_QUERY = 256
D_HALF = D_QUERY // 2
TOKEN_BLOCK = 128
EPS = 1e-6

kernel_name = 'hybrid_conv_sgu_peer_decode_step'


def rms_norm(x, g):
    x32 = x.astype(jnp.float32)
    y = x32 * lax.rsqrt(jnp.mean(x32 * x32, axis=-1, keepdims=True) + EPS)
    return (y * g.astype(jnp.float32)).astype(x.dtype)


def group_layer_norm(v, g):
    v32 = v.astype(jnp.float32)
    mu = jnp.mean(v32, axis=-1, keepdims=True)
    d = v32 - mu
    y = d * lax.rsqrt(jnp.mean(d * d, axis=-1, keepdims=True) + EPS)
    return (y * g.astype(jnp.float32).reshape(SGU_GROUPS, SGU_GROUP_DIM)).astype(v.dtype)


def ada_params(c, w_ada, b_ada):
    m = jnp.einsum('bd,de->be', jax.nn.silu(c), w_ada) + b_ada
    return jnp.split(m[:, None, :], 6, axis=-1)


def short_conv_mixer(b_gate, c_gate, h, prefix, conv_w):
    z = c_gate * h
    zp = jnp.concatenate([prefix.astype(z.dtype), z], axis=1)
    L = z.shape[1]
    y = zp[:, 0:L] * conv_w[:, 0]
    for k in range(1, CONV_W):
        y = y + zp[:, k:k + L] * conv_w[:, k]
    return b_gate * y, zp[:, -(CONV_W - 1):]


def chunk_sgu(u, v, sgu_norm_g, w_s, b_s):
    bn, L, _ = v.shape
    vn = group_layer_norm(v.reshape(bn, L, SGU_GROUPS, SGU_GROUP_DIM), sgu_norm_g)
    n_chunks = -(-L // CHUNK)
    pad = n_chunks * CHUNK - L
    vp = jnp.pad(vn, ((0, 0), (0, pad), (0, 0), (0, 0))).reshape(
        bn, n_chunks, CHUNK, SGU_GROUPS, SGU_GROUP_DIM)
    w_causal = jnp.tril(w_s)
    mixed = jnp.einsum('gts,bnsgc->bntgc', w_causal, vp) + b_s.T[None, None, :, :, None]
    mixed = mixed.reshape(bn, n_chunks * CHUNK, SGU_DIM)[:, :L]
    return u * mixed, vn.reshape(bn, L, SGU_DIM)


def peer_block(xb, w_q, sub_keys, expert_u, expert_v):
    tb = xb.shape[0]
    q = jnp.einsum('td,de->te', xb, w_q).reshape(tb, PEER_HEADS, 2, D_HALF)
    s = jnp.einsum('thpc,hpkc->thpk', q, sub_keys).astype(jnp.float32)
    sv, si = lax.top_k(s, PEER_TOPK)
    cand = sv[:, :, 0, :, None] + sv[:, :, 1, None, :]
    cv, ci = lax.top_k(cand.reshape(tb, PEER_HEADS, PEER_TOPK * PEER_TOPK), PEER_TOPK)
    i1 = jnp.take_along_axis(si[:, :, 0, :], ci // PEER_TOPK, axis=-1)
    i2 = jnp.take_along_axis(si[:, :, 1, :], ci % PEER_TOPK, axis=-1)
    expert = (i1 * N_KEYS + i2).reshape(tb, PEER_HEADS * PEER_TOPK)
    g = jax.nn.softmax(cv, axis=-1).reshape(tb, PEER_HEADS * PEER_TOPK)
    act = jax.nn.gelu(jnp.einsum('tkd,td->tk', expert_u[expert], xb).astype(jnp.float32),
                      approximate=False)
    w = (g * act).astype(xb.dtype)
    return jnp.einsum('tk,tkd->td', w, expert_v[expert])


def peer_ffn(x, w_q, sub_keys, expert_u, expert_v):
    bn, L, d = x.shape
    t = bn * L
    nb = -(-t // TOKEN_BLOCK)
    xt = jnp.pad(x.reshape(t, d), ((0, nb * TOKEN_BLOCK - t), (0, 0))).reshape(nb, TOKEN_BLOCK, d)
    y = lax.map(lambda xb: peer_block(xb, w_q, sub_keys, expert_u, expert_v), xt)
    return y.reshape(nb * TOKEN_BLOCK, d)[:t].reshape(bn, L, d)


def layer(x, c, conv_prefix, w_ada, b_ada, g_pre_mix, w_in, conv_w, sgu_norm_g, w_s, b_s,
          w_out, g_post_mix, g_pre_ffn, w_q, sub_keys, expert_u, expert_v, g_post_ffn):
    sh1, sc1, gt1, sh2, sc2, gt2 = ada_params(c, w_ada, b_ada)
    h = rms_norm(x, g_pre_mix) * (1 + sc1) + sh1
    p = jnp.einsum('bld,de->ble', h, w_in)
    b_gate, c_gate, hc, u, v = jnp.split(
        p, [CONV_DIM, 2 * CONV_DIM, 3 * CONV_DIM, 3 * CONV_DIM + SGU_DIM], axis=-1)
    yc, conv_state = short_conv_mixer(b_gate, c_gate, hc, conv_prefix, conv_w)
    ys, v_rows = chunk_sgu(u, v, sgu_norm_g, w_s, b_s)
    mix = jnp.einsum('ble,ed->bld', jnp.concatenate([yc, ys], axis=-1), w_out)
    x = x + gt1 * rms_norm(mix, g_post_mix)
    h2 = rms_norm(x, g_pre_ffn) * (1 + sc2) + sh2
    x = x + gt2 * rms_norm(peer_ffn(h2, w_q, sub_keys, expert_u, expert_v), g_post_ffn)
    return x, conv_state, v_rows


def setup_inputs(seed: int = 0) -> dict:
    key = jax.random.key(seed)
    ks = jax.random.split(key, 24)

    def nrm(k, shape, scale):
        return jax.random.normal(k, shape, jnp.float32) * scale

    return {
        'x_prompt': nrm(ks[0], (BATCH, SEQ, D_MODEL), 1.0),
        'x_sample': nrm(ks[1], (DEC_BATCH, DEC_SEQ, D_MODEL), 1.0),
        'state_conv': nrm(ks[2], (DEPTH, DEC_BATCH, CONV_W - 1, CONV_DIM), 1.0),
        'c_prompt': nrm(ks[3], (BATCH, D_MODEL), 1.0),
        'c_sample': nrm(ks[4], (DEC_BATCH, D_MODEL), 1.0),
        'w_ada': nrm(ks[5], (DEPTH, D_MODEL, 6 * D_MODEL), 0.5 * D_MODEL ** -0.5),
        'b_ada': nrm(ks[6], (DEPTH, 6 * D_MODEL), 0.01),
        'g_pre_mix': 1.0 + nrm(ks[7], (DEPTH, D_MODEL), 0.05),
        'w_in': nrm(ks[8], (DEPTH, D_MODEL, IN_DIM), D_MODEL ** -0.5),
        'conv_w': nrm(ks[9], (DEPTH, CONV_DIM, CONV_W), CONV_W ** -0.5),
        'sgu_norm_g': 1.0 + nrm(ks[10], (DEPTH, SGU_DIM), 0.05),
        'w_s': nrm(ks[11], (DEPTH, SGU_GROUPS, CHUNK, CHUNK), CHUNK ** -0.5),
        'b_s': 1.0 + nrm(ks[12], (DEPTH, SGU_GROUPS, CHUNK), 0.1),
        'w_out': nrm(ks[13], (DEPTH, D_MODEL, D_MODEL), D_MODEL ** -0.5),
        'g_post_mix': 1.0 + nrm(ks[14], (DEPTH, D_MODEL), 0.05),
        'g_pre_ffn': 1.0 + nrm(ks[15], (DEPTH, D_MODEL), 0.05),
        'w_q': nrm(ks[16], (DEPTH, D_MODEL, PEER_HEADS * D_QUERY), D_MODEL ** -0.5),
        'sub_keys': nrm(ks[17], (DEPTH, PEER_HEADS, 2, N_KEYS, D_HALF), D_HALF ** -0.5),
        'expert_u': nrm(ks[18], (DEPTH, N_EXPERTS, D_MODEL), D_MODEL ** -0.5),
        'expert_v': nrm(ks[19], (DEPTH, N_EXPERTS, D_MODEL), D_MODEL ** -0.5),
        'g_post_ffn': 1.0 + nrm(ks[20], (DEPTH, D_MODEL), 0.05),
    }


def reference(x_prompt, x_sample, state_conv, c_prompt, c_sample, w_ada, b_ada, g_pre_mix,
              w_in, conv_w, sgu_norm_g, w_s, b_s, w_out, g_post_mix, g_pre_ffn, w_q,
              sub_keys, expert_u, expert_v, g_post_ffn):
    xp, xs = x_prompt, x_sample
    conv_p, conv_s, v_s = [], [], []
    for l in range(DEPTH):
        params = (w_ada[l], b_ada[l], g_pre_mix[l], w_in[l], conv_w[l], sgu_norm_g[l], w_s[l],
                  b_s[l], w_out[l], g_post_mix[l], g_pre_ffn[l], w_q[l], sub_keys[l],
                  expert_u[l], expert_v[l], g_post_ffn[l])
        zero_prefix = jnp.zeros((xp.shape[0], CONV_W - 1, CONV_DIM), xp.dtype)
        xp, cst_p, _ = layer(xp, c_prompt, zero_prefix, *params)
        xs, cst_s, v_rows_s = layer(xs, c_sample, state_conv[l], *params)
        conv_p.append(cst_p)
        conv_s.append(cst_s)
        v_s.append(v_rows_s)
    new_conv_prompt = jnp.stack(conv_p)
    new_conv_sample = jnp.stack(conv_s)
    sgu_v_sample = jnp.stack(v_s)
    return (xp, xs, new_conv_prompt, new_conv_sample, sgu_v_sample)
```

```python
import functools

import numpy as np
import jax
import jax.numpy as jnp
from jax import lax
from jax.experimental import pallas as pl
from jax.experimental.pallas import tpu as pltpu

EPS = 1e-6
CONV_W = 3
SGU_GROUPS = 4
CHUNK = 128
PEER_HEADS = 8
N_KEYS = 128
PEER_TOPK = 16
LANES = 128
SUBLANES = 8
VMEM_LIMIT_BYTES = 56 * 1024 * 1024

_NT = (((1,), (1,)), ((), ()))


def _rms(x, g):
    return x * lax.rsqrt(jnp.mean(x * x, axis=-1, keepdims=True) + EPS) * g


MXU_DTYPE = jnp.bfloat16


def _bf16(x):
    return x.astype(MXU_DTYPE)


def _ada_kernel(c_ref, w_ref, b_ref, o_ref):
    c = c_ref[...]
    s = c * (1.0 / (1.0 + jnp.exp(-c)))
    o_ref[...] = jnp.dot(_bf16(s), w_ref[...], preferred_element_type=jnp.float32) + b_ref[...]


def _ada(c, w_ada, b_ada):
    n, d = c.shape
    e = w_ada.shape[1]
    tn = d
    return pl.pallas_call(
        _ada_kernel,
        grid=(e // tn,),
        in_specs=[
            pl.BlockSpec((n, d), lambda j: (0, 0)),
            pl.BlockSpec((d, tn), lambda j: (0, j)),
            pl.BlockSpec((1, tn), lambda j: (0, j)),
        ],
        out_specs=pl.BlockSpec((n, tn), lambda j: (0, j)),
        out_shape=jax.ShapeDtypeStruct((n, e), jnp.float32),
        name="ada",
    )(c, w_ada, b_ada)


def _group_norm(v, gs, gdim):
    outs = []
    for g in range(SGU_GROUPS):
        vg = v[:, g * gdim:(g + 1) * gdim]
        mu = jnp.mean(vg, axis=-1, keepdims=True)
        d = vg - mu
        yn = d * lax.rsqrt(jnp.mean(d * d, axis=-1, keepdims=True) + EPS)
        outs.append(yn * gs[:, g * gdim:(g + 1) * gdim])
    return jnp.concatenate(outs, axis=-1)


def _mixer_tail(x, yc, ys, mod, d, w_out_ref, g_post_ref, g_ffn_ref, x1_ref, h2_ref):
    gt1 = mod[:, 2 * d:3 * d]
    sh2 = mod[:, 3 * d:4 * d]
    sc2 = mod[:, 4 * d:5 * d]
    mix_in = _bf16(jnp.concatenate([yc, ys], axis=-1))
    mix = jnp.dot(mix_in, w_out_ref[...], preferred_element_type=jnp.float32)
    x1 = x + gt1 * _rms(mix, g_post_ref[...])
    h2 = _rms(x1, g_ffn_ref[...]) * (1.0 + sc2) + sh2
    return x1, h2


def _mixer_prompt_kernel(x_ref, mod_ref, g_pre_ref, w_in_ref, cw_ref, gs_ref, ws_ref, bias_ref,
                         w_out_ref, g_post_ref, g_ffn_ref,
                         x1_ref, h2_ref, conv_ref, carry_ref, *, d, cdim, sdim):
    tm = x_ref.shape[1]
    gdim = sdim // SGU_GROUPS

    @pl.when(pl.program_id(1) == 0)
    def _():
        carry_ref[...] = jnp.zeros_like(carry_ref)

    x = x_ref[0]
    mod = mod_ref[0]
    sh1 = mod[:, 0:d]
    sc1 = mod[:, d:2 * d]
    h = _rms(x, g_pre_ref[...]) * (1.0 + sc1) + sh1
    p = jnp.dot(_bf16(h), w_in_ref[...], preferred_element_type=jnp.float32)
    b_gate = p[:, 0:cdim]
    z = p[:, cdim:2 * cdim] * p[:, 2 * cdim:3 * cdim]
    u = p[:, 3 * cdim:3 * cdim + sdim]
    v = p[:, 3 * cdim + sdim:]

    zext = jnp.concatenate([carry_ref[...], z], axis=0)
    z1 = pltpu.roll(zext, 1, 0)[SUBLANES:]
    z2 = pltpu.roll(zext, 2, 0)[SUBLANES:]
    cw = cw_ref[...]
    y = z2 * cw[0:1] + z1 * cw[1:2] + z * cw[2:3]
    yc = b_gate * y
    carry_ref[...] = z[tm - SUBLANES:]
    conv_ref[0] = z[tm - (CONV_W - 1):]

    vn = _group_norm(v, gs_ref[...], gdim)
    vnb = _bf16(vn)
    row = lax.broadcasted_iota(jnp.int32, (CHUNK, CHUNK), 0)
    col = lax.broadcasted_iota(jnp.int32, (CHUNK, CHUNK), 1)
    bias = bias_ref[...]
    chunks = []
    for ch in range(tm // CHUNK):
        groups = []
        for g in range(SGU_GROUPS):
            wt = _bf16(jnp.where(row >= col, ws_ref[g], 0.0))
            groups.append(jnp.dot(wt, vnb[ch * CHUNK:(ch + 1) * CHUNK, g * gdim:(g + 1) * gdim],
                                  preferred_element_type=jnp.float32))
        chunks.append(jnp.concatenate(groups, axis=-1) + bias)
    mixed = jnp.concatenate(chunks, axis=0)
    ys = u * mixed

    x1, h2 = _mixer_tail(x, yc, ys, mod, d, w_out_ref, g_post_ref, g_ffn_ref, x1_ref, h2_ref)
    x1_ref[0] = x1
    h2_ref[0] = _bf16(h2)


def _mixer_sample_kernel(x_ref, mod_ref, st_ref, g_pre_ref, w_in_ref, cw_ref, gs_ref, w00_ref, b0_ref,
                         w_out_ref, g_post_ref, g_ffn_ref,
                         x1_ref, h2_ref, conv_ref, vn_ref, *, d, cdim, sdim):
    gdim = sdim // SGU_GROUPS
    x = x_ref[...]
    mod = mod_ref[...]
    sh1 = mod[:, 0:d]
    sc1 = mod[:, d:2 * d]
    h = _rms(x, g_pre_ref[...]) * (1.0 + sc1) + sh1
    p = jnp.dot(_bf16(h), w_in_ref[...], preferred_element_type=jnp.float32)
    b_gate = p[:, 0:cdim]
    z = p[:, cdim:2 * cdim] * p[:, 2 * cdim:3 * cdim]
    u = p[:, 3 * cdim:3 * cdim + sdim]
    v = p[:, 3 * cdim + sdim:]

    st = st_ref[...]
    z2 = st[:, 0:cdim]
    z1 = st[:, cdim:]
    cw = cw_ref[...]
    y = z2 * cw[0:1] + z1 * cw[1:2] + z * cw[2:3]
    yc = b_gate * y
    conv_ref[...] = jnp.concatenate([z1, z], axis=-1)

    vn = _group_norm(v, gs_ref[...], gdim)
    vn_ref[...] = vn
    mixed = vn * w00_ref[...] + b0_ref[...]
    ys = u * mixed

    x1, h2 = _mixer_tail(x, yc, ys, mod, d, w_out_ref, g_post_ref, g_ffn_ref, x1_ref, h2_ref)
    x1_ref[...] = x1
    h2_ref[...] = _bf16(h2)


def _const_spec(shape):
    zeros = (0,) * len(shape)
    return pl.BlockSpec(shape, lambda *_: zeros)


def _mixer_prompt(x, mod, g_pre, w_in, cw, gs, ws, bias, w_out, g_post, g_ffn, *, tm):
    b, l, d = x.shape
    cdim = cw.shape[1]
    sdim = gs.shape[1]
    kern = functools.partial(_mixer_prompt_kernel, d=d, cdim=cdim, sdim=sdim)
    tok = lambda i, s: (i, s, 0)
    return pl.pallas_call(
        kern,
        grid=(b, l // tm),
        in_specs=[
            pl.BlockSpec((1, tm, d), tok),
            pl.BlockSpec((1, 1, mod.shape[2]), lambda i, s: (i, 0, 0)),
            _const_spec(g_pre.shape), _const_spec(w_in.shape), _const_spec(cw.shape),
            _const_spec(gs.shape), _const_spec(ws.shape), _const_spec(bias.shape),
            _const_spec(w_out.shape), _const_spec(g_post.shape), _const_spec(g_ffn.shape),
        ],
        out_specs=[
            pl.BlockSpec((1, tm, d), tok),
            pl.BlockSpec((1, tm, d), tok),
            pl.BlockSpec((1, CONV_W - 1, cdim), lambda i, s: (i, 0, 0)),
        ],
        out_shape=[
            jax.ShapeDtypeStruct((b, l, d), jnp.float32),
            jax.ShapeDtypeStruct((b, l, d), MXU_DTYPE),
            jax.ShapeDtypeStruct((b, CONV_W - 1, cdim), jnp.float32),
        ],
        scratch_shapes=[pltpu.VMEM((SUBLANES, cdim), jnp.float32)],
        compiler_params=pltpu.CompilerParams(
            dimension_semantics=("arbitrary", "arbitrary"), vmem_limit_bytes=VMEM_LIMIT_BYTES),
        name="mixer_prompt",
    )(x, mod, g_pre, w_in, cw, gs, ws, bias, w_out, g_post, g_ffn)


def _mixer_sample(x, mod, st, g_pre, w_in, cw, gs, w00, b0, w_out, g_post, g_ffn):
    n, d = x.shape
    cdim = cw.shape[1]
    sdim = gs.shape[1]
    kern = functools.partial(_mixer_sample_kernel, d=d, cdim=cdim, sdim=sdim)
    args = (x, mod, st, g_pre, w_in, cw, gs, w00, b0, w_out, g_post, g_ffn)
    return pl.pallas_call(
        kern,
        grid=(1,),
        in_specs=[_const_spec(a.shape) for a in args],
        out_specs=[_const_spec((n, d)), _const_spec((n, d)),
                   _const_spec((n, (CONV_W - 1) * cdim)), _const_spec((n, sdim))],
        out_shape=[
            jax.ShapeDtypeStruct((n, d), jnp.float32),
            jax.ShapeDtypeStruct((n, d), MXU_DTYPE),
            jax.ShapeDtypeStruct((n, (CONV_W - 1) * cdim), jnp.float32),
            jax.ShapeDtypeStruct((n, sdim), jnp.float32),
        ],
        compiler_params=pltpu.CompilerParams(vmem_limit_bytes=VMEM_LIMIT_BYTES),
        name="mixer_sample",
    )(*args)


def _candidate_cells():
    k = PEER_TOPK
    idx = np.full((10 * SUBLANES,), 1e9, np.float32)
    idx[0:k] = np.arange(k)
    idx[k:k + 8] = k + np.arange(8)
    for a in range(2, 8):
        nb = k // (a + 1)
        base = 24 + 8 * (a - 2)
        idx[base:base + nb] = a * k + np.arange(nb)
    idx[72:80] = (8 + np.arange(8)) * k
    return np.broadcast_to(idx[:, None], (idx.shape[0], LANES)).copy()


def _topk_ranks(s):
    iota = lax.broadcasted_iota(jnp.int32, s.shape, 0).astype(jnp.float32)
    iota16 = lax.broadcasted_iota(jnp.int32, (PEER_TOPK, s.shape[1]), 0)
    rank = jnp.full(s.shape, float(PEER_TOPK), jnp.float32)
    sv = jnp.zeros((PEER_TOPK, s.shape[1]), jnp.float32)
    work = s
    for a in range(PEER_TOPK):
        mx = jnp.max(work, axis=0, keepdims=True)
        idx = jnp.min(jnp.where(work == mx, iota, float(N_KEYS)), axis=0, keepdims=True)
        hit = iota == idx
        rank = jnp.where(hit, float(a), rank)
        work = jnp.where(hit, -jnp.inf, work)
        sv = jnp.where(iota16 == a, mx, sv)
    return rank, sv


def _route_head(s1, s2, cidx):
    rank1, sv1 = _topk_ranks(s1)
    rank2, sv2 = _topk_ranks(s2)
    pieces = [sv2 + sv1[0:1], sv2[0:8] + sv1[1:2]]
    for a in range(2, 8):
        pieces.append(sv2[0:8] + sv1[a:a + 1])
    pieces.append(sv1[8:16] + sv2[0:1])
    cand = jnp.concatenate(pieces, axis=0)
    cand = jnp.where(cidx < 1e8, cand, -jnp.inf)
    sel = jnp.zeros(cand.shape, jnp.float32)
    work = cand
    for _ in range(PEER_TOPK):
        mx = jnp.max(work, axis=0, keepdims=True)
        idx = jnp.min(jnp.where(work == mx, cidx, 1e9), axis=0, keepdims=True)
        hit = cidx == idx
        sel = jnp.where(hit, 1.0, sel)
        work = jnp.where(hit, -jnp.inf, work)
    m = cand[0:1]
    z = jnp.sum(sel * jnp.exp(cand - m), axis=0, keepdims=True)
    n1 = jnp.zeros(s1.shape, jnp.float32)
    n1 = jnp.where(rank1 == 0.0, jnp.sum(sel[0:16], axis=0, keepdims=True), n1)
    for a in range(1, 8):
        base = 16 + 8 * (a - 1)
        n1 = jnp.where(rank1 == float(a), jnp.sum(sel[base:base + 8], axis=0, keepdims=True), n1)
    for a in range(8, 16):
        n1 = jnp.where(rank1 == float(a), sel[72 + a - 8:73 + a - 8], n1)
    a1 = jnp.exp(s1 - sv1[0:1]) / z
    b2 = jnp.exp(s2 - sv2[0:1])
    return n1, a1, rank2, b2


def _peer_kernel(h2_ref, x1_ref, gt2_ref, wqt_ref, keys_ref, cidx_ref, g_post_ref, u_ref, vt_ref,
                 y_ref,
                 qt_s, n1_s, a1_s, r2_s, b2_s, a_s, w_s, acc_s):
    j = pl.program_id(1)
    tb = h2_ref.shape[0]
    ec = u_ref.shape[0]
    rows_per_step = ec // N_KEYS
    n_sub = tb // LANES

    @pl.when(j == 0)
    def _route():
        qt = lax.dot_general(wqt_ref[...], h2_ref[...], _NT, preferred_element_type=jnp.float32)
        qt_s[...] = _bf16(qt)
        acc_s[...] = jnp.zeros_like(acc_s)

        def sub_body(t, carry):
            lanes = pl.ds(pl.multiple_of(t * LANES, LANES), LANES)
            cidx = cidx_ref[...]
            for h in range(PEER_HEADS):
                s = []
                for p in range(2):
                    hp = 2 * h + p
                    s.append(jnp.dot(keys_ref[hp], qt_s[hp * N_KEYS:(hp + 1) * N_KEYS, lanes],
                                     preferred_element_type=jnp.float32))
                n1, a1, r2, b2 = _route_head(s[0], s[1], cidx)
                n1_s[h, :, lanes] = n1
                a1_s[h, :, lanes] = a1
                r2_s[h, :, lanes] = r2
                b2_s[h, :, lanes] = b2
            return carry

        lax.fori_loop(0, n_sub, sub_body, 0)

    a_s[...] = lax.dot_general(u_ref[...], h2_ref[...], _NT, preferred_element_type=jnp.float32)

    def tile_body(it, carry):
        rg = it // n_sub
        t = it - rg * n_sub
        lanes = pl.ds(pl.multiple_of(t * LANES, LANES), LANES)
        c0 = pl.multiple_of(j * rows_per_step + rg * SUBLANES, SUBLANES)
        for r in range(SUBLANES):
            rows = pl.ds(pl.multiple_of((rg * SUBLANES + r) * N_KEYS, N_KEYS), N_KEYS)
            a = a_s[rows, lanes]
            act = 0.5 * a * (1.0 + lax.erf(a * np.float32(np.sqrt(0.5))))
            gate = jnp.zeros((N_KEYS, LANES), jnp.float32)
            for h in range(PEER_HEADS):
                n1 = n1_s[h, pl.ds(c0, SUBLANES), lanes][r:r + 1]
                a1 = a1_s[h, pl.ds(c0, SUBLANES), lanes][r:r + 1]
                gate = gate + jnp.where(r2_s[h, :, lanes] < n1, b2_s[h, :, lanes] * a1, 0.0)
            w_s[rows, lanes] = _bf16(act * gate)
        return carry

    lax.fori_loop(0, (rows_per_step // SUBLANES) * n_sub, tile_body, 0)

    acc_s[...] += jnp.dot(vt_ref[...], w_s[...], preferred_element_type=jnp.float32)

    @pl.when(j == pl.num_programs(1) - 1)
    def _finish():
        ffn = acc_s[...].T
        y_ref[...] = x1_ref[...] + gt2_ref[0] * _rms(ffn, g_post_ref[...])


def _peer(h2, x1, mod, wqt, keys, cidx, g_post, u_bf, vt_bf, *, tb, blocks_per_mod, ec):
    t, d = h2.shape
    ne = u_bf.shape[0]
    mod_rows = mod.shape[1]
    n_gt2 = 5
    return pl.pallas_call(
        _peer_kernel,
        grid=(t // tb, ne // ec),
        in_specs=[
            pl.BlockSpec((tb, d), lambda i, j: (i, 0)),
            pl.BlockSpec((tb, d), lambda i, j: (i, 0)),
            pl.BlockSpec((1, mod_rows, d), lambda i, j: (i // blocks_per_mod, 0, n_gt2)),
            _const_spec(wqt.shape), _const_spec(keys.shape), _const_spec(cidx.shape),
            _const_spec(g_post.shape),
            pl.BlockSpec((ec, d), lambda i, j: (j, 0)),
            pl.BlockSpec((d, ec), lambda i, j: (0, j)),
        ],
        out_specs=pl.BlockSpec((tb, d), lambda i, j: (i, 0)),
        out_shape=jax.ShapeDtypeStruct((t, d), jnp.float32),
        scratch_shapes=[
            pltpu.VMEM((wqt.shape[0], tb), MXU_DTYPE),
            pltpu.VMEM((PEER_HEADS, N_KEYS, tb), jnp.float32),
            pltpu.VMEM((PEER_HEADS, N_KEYS, tb), jnp.float32),
            pltpu.VMEM((PEER_HEADS, N_KEYS, tb), jnp.float32),
            pltpu.VMEM((PEER_HEADS, N_KEYS, tb), jnp.float32),
            pltpu.VMEM((ec, tb), jnp.float32),
            pltpu.VMEM((ec, tb), MXU_DTYPE),
            pltpu.VMEM((d, tb), jnp.float32),
        ],
        compiler_params=pltpu.CompilerParams(
            dimension_semantics=("arbitrary", "arbitrary"), vmem_limit_bytes=VMEM_LIMIT_BYTES),
        name="peer",
    )(h2, x1, mod, wqt, keys, cidx, g_post, u_bf, vt_bf)


def kernel(x_prompt, x_sample, state_conv, c_prompt, c_sample, w_ada, b_ada, g_pre_mix, w_in, conv_w,
           sgu_norm_g, w_s, b_s, w_out, g_post_mix, g_pre_ffn, w_q, sub_keys, expert_u, expert_v,
           g_post_ffn):
    depth = w_ada.shape[0]
    assert depth == 1, "single-layer step"
    b, l, d = x_prompt.shape
    nb = x_sample.shape[0]
    assert x_sample.shape[1] == 1 and l % CHUNK == 0
    cdim = conv_w.shape[1]
    sdim = sgu_norm_g.shape[1]
    gdim = sdim // SGU_GROUPS
    assert gdim == LANES and sub_keys.shape[3] == N_KEYS and sub_keys.shape[1] == PEER_HEADS
    ne = expert_u.shape[1]
    assert ne == N_KEYS * N_KEYS

    row = lambda a: a.reshape(1, -1)
    w_ada_b = _bf16(w_ada[0])
    w_in_b = _bf16(w_in[0])
    w_out_b = _bf16(w_out[0])
    wqt_b = _bf16(w_q[0].T)
    keys_b = _bf16(sub_keys[0].reshape(2 * PEER_HEADS, N_KEYS, -1))
    u_b = _bf16(expert_u[0])
    vt_b = _bf16(expert_v[0].T)
    cw = conv_w[0].T
    bias = jnp.repeat(b_s[0].T, gdim, axis=1)
    w00 = jnp.repeat(w_s[0][:, 0, 0], gdim).reshape(1, sdim)
    b0 = bias[0:1]
    cidx = jnp.asarray(_candidate_cells())

    c_all = jnp.concatenate([c_prompt, c_sample], axis=0)
    mod = _ada(c_all, w_ada_b, row(b_ada[0]))
    mod_p = mod[:b].reshape(b, 1, -1)
    mod_s = mod[b:]

    tm = min(512, l)
    x1p, h2p, conv_p = _mixer_prompt(
        x_prompt, mod_p, row(g_pre_mix[0]), w_in_b, cw, row(sgu_norm_g[0]), w_s[0], bias, w_out_b,
        row(g_post_mix[0]), row(g_pre_ffn[0]), tm=tm)
    x1s, h2s, conv_s, vn_s = _mixer_sample(
        x_sample.reshape(nb, d), mod_s, state_conv[0].reshape(nb, -1), row(g_pre_mix[0]), w_in_b, cw,
        row(sgu_norm_g[0]), w00, b0, w_out_b, row(g_post_mix[0]), row(g_pre_ffn[0]))

    tb = min(512, l)
    ec = 1024
    yp = _peer(h2p.reshape(b * l, d), x1p.reshape(b * l, d), mod_p, wqt_b, keys_b, cidx,
               row(g_post_ffn[0]), u_b, vt_b, tb=tb, blocks_per_mod=l // tb, ec=ec)
    ys = _peer(h2s, x1s, mod_s.reshape(1, nb, -1), wqt_b, keys_b, cidx,
               row(g_post_ffn[0]), u_b, vt_b, tb=nb, blocks_per_mod=1, ec=ec)

    return (yp.reshape(b, l, d), ys.reshape(nb, 1, d),
            conv_p.reshape(1, b, CONV_W - 1, cdim), conv_s.reshape(1, nb, CONV_W - 1, cdim),
            vn_s.reshape(1, nb, 1, sdim))
```

```python
import functools

import numpy as np
import jax
import jax.numpy as jnp
from jax import lax
from jax.experimental import pallas as pl
from jax.experimental.pallas import tpu as pltpu

EPS = 1e-6
CONV_W = 3
SGU_GROUPS = 4
CHUNK = 128
PEER_HEADS = 8
N_KEYS = 128
PEER_TOPK = 16
LANES = 128
SUBLANES = 8
VMEM_LIMIT_BYTES = 56 * 1024 * 1024

_NT = (((1,), (1,)), ((), ()))


def _rms(x, g):
    return x * lax.rsqrt(jnp.mean(x * x, axis=-1, keepdims=True) + EPS) * g


MXU_DTYPE = jnp.bfloat16
GATE_DTYPE = jnp.bfloat16


def _bf16(x):
    return x.astype(MXU_DTYPE)


def _ada_kernel(c_ref, w_ref, b_ref, o_ref):
    c = c_ref[...]
    s = c * (1.0 / (1.0 + jnp.exp(-c)))
    o_ref[...] = jnp.dot(_bf16(s), w_ref[...], preferred_element_type=jnp.float32) + b_ref[...]


def _ada(c, w_ada, b_ada):
    n, d = c.shape
    e = w_ada.shape[1]
    tn = d
    return pl.pallas_call(
        _ada_kernel,
        grid=(e // tn,),
        in_specs=[
            pl.BlockSpec((n, d), lambda j: (0, 0)),
            pl.BlockSpec((d, tn), lambda j: (0, j)),
            pl.BlockSpec((1, tn), lambda j: (0, j)),
        ],
        out_specs=pl.BlockSpec((n, tn), lambda j: (0, j)),
        out_shape=jax.ShapeDtypeStruct((n, e), jnp.float32),
        name="ada",
    )(c, w_ada, b_ada)


def _group_norm(v, gs, gdim):
    outs = []
    for g in range(SGU_GROUPS):
        vg = v[:, g * gdim:(g + 1) * gdim]
        mu = jnp.mean(vg, axis=-1, keepdims=True)
        d = vg - mu
        yn = d * lax.rsqrt(jnp.mean(d * d, axis=-1, keepdims=True) + EPS)
        outs.append(yn * gs[:, g * gdim:(g + 1) * gdim])
    return jnp.concatenate(outs, axis=-1)


def _mixer_tail(x, yc, ys, mod, d, w_out_ref, g_post_ref, g_ffn_ref, x1_ref, h2_ref):
    gt1 = mod[:, 2 * d:3 * d]
    sh2 = mod[:, 3 * d:4 * d]
    sc2 = mod[:, 4 * d:5 * d]
    mix_in = _bf16(jnp.concatenate([yc, ys], axis=-1))
    mix = jnp.dot(mix_in, w_out_ref[...], preferred_element_type=jnp.float32)
    x1 = x + gt1 * _rms(mix, g_post_ref[...])
    h2 = _rms(x1, g_ffn_ref[...]) * (1.0 + sc2) + sh2
    return x1, h2


def _mixer_prompt_kernel(x_ref, mod_ref, g_pre_ref, w_in_ref, cw_ref, gs_ref, ws_ref, bias_ref,
                         w_out_ref, g_post_ref, g_ffn_ref,
                         x1_ref, h2_ref, conv_ref, carry_ref, *, d, cdim, sdim):
    tm = x_ref.shape[1]
    gdim = sdim // SGU_GROUPS

    @pl.when(pl.program_id(1) == 0)
    def _():
        carry_ref[...] = jnp.zeros_like(carry_ref)

    x = x_ref[0]
    mod = mod_ref[0]
    sh1 = mod[:, 0:d]
    sc1 = mod[:, d:2 * d]
    h = _rms(x, g_pre_ref[...]) * (1.0 + sc1) + sh1
    p = jnp.dot(_bf16(h), w_in_ref[...], preferred_element_type=jnp.float32)
    b_gate = p[:, 0:cdim]
    z = p[:, cdim:2 * cdim] * p[:, 2 * cdim:3 * cdim]
    u = p[:, 3 * cdim:3 * cdim + sdim]
    v = p[:, 3 * cdim + sdim:]

    zext = jnp.concatenate([carry_ref[...], z], axis=0)
    z1 = pltpu.roll(zext, 1, 0)[SUBLANES:]
    z2 = pltpu.roll(zext, 2, 0)[SUBLANES:]
    cw = cw_ref[...]
    y = z2 * cw[0:1] + z1 * cw[1:2] + z * cw[2:3]
    yc = b_gate * y
    carry_ref[...] = z[tm - SUBLANES:]
    conv_ref[0] = z[tm - (CONV_W - 1):]

    vn = _group_norm(v, gs_ref[...], gdim)
    vnb = _bf16(vn)
    row = lax.broadcasted_iota(jnp.int32, (CHUNK, CHUNK), 0)
    col = lax.broadcasted_iota(jnp.int32, (CHUNK, CHUNK), 1)
    bias = bias_ref[...]
    chunks = []
    for ch in range(tm // CHUNK):
        groups = []
        for g in range(SGU_GROUPS):
            wt = _bf16(jnp.where(row >= col, ws_ref[g], 0.0))
            groups.append(jnp.dot(wt, vnb[ch * CHUNK:(ch + 1) * CHUNK, g * gdim:(g + 1) * gdim],
                                  preferred_element_type=jnp.float32))
        chunks.append(jnp.concatenate(groups, axis=-1) + bias)
    mixed = jnp.concatenate(chunks, axis=0)
    ys = u * mixed

    x1, h2 = _mixer_tail(x, yc, ys, mod, d, w_out_ref, g_post_ref, g_ffn_ref, x1_ref, h2_ref)
    x1_ref[0] = x1
    h2_ref[0] = _bf16(h2)


def _mixer_sample_kernel(x_ref, mod_ref, st_ref, g_pre_ref, w_in_ref, cw_ref, gs_ref, w00_ref, b0_ref,
                         w_out_ref, g_post_ref, g_ffn_ref,
                         x1_ref, h2_ref, conv_ref, vn_ref, *, d, cdim, sdim):
    gdim = sdim // SGU_GROUPS
    x = x_ref[...]
    mod = mod_ref[...]
    sh1 = mod[:, 0:d]
    sc1 = mod[:, d:2 * d]
    h = _rms(x, g_pre_ref[...]) * (1.0 + sc1) + sh1
    p = jnp.dot(_bf16(h), w_in_ref[...], preferred_element_type=jnp.float32)
    b_gate = p[:, 0:cdim]
    z = p[:, cdim:2 * cdim] * p[:, 2 * cdim:3 * cdim]
    u = p[:, 3 * cdim:3 * cdim + sdim]
    v = p[:, 3 * cdim + sdim:]

    st = st_ref[...]
    z2 = st[:, 0:cdim]
    z1 = st[:, cdim:]
    cw = cw_ref[...]
    y = z2 * cw[0:1] + z1 * cw[1:2] + z * cw[2:3]
    yc = b_gate * y
    conv_ref[...] = jnp.concatenate([z1, z], axis=-1)

    vn = _group_norm(v, gs_ref[...], gdim)
    vn_ref[...] = vn
    mixed = vn * w00_ref[...] + b0_ref[...]
    ys = u * mixed

    x1, h2 = _mixer_tail(x, yc, ys, mod, d, w_out_ref, g_post_ref, g_ffn_ref, x1_ref, h2_ref)
    x1_ref[...] = x1
    h2_ref[...] = _bf16(h2)


def _const_spec(shape):
    zeros = (0,) * len(shape)
    return pl.BlockSpec(shape, lambda *_: zeros)


def _mixer_prompt(x, mod, g_pre, w_in, cw, gs, ws, bias, w_out, g_post, g_ffn, *, tm):
    b, l, d = x.shape
    cdim = cw.shape[1]
    sdim = gs.shape[1]
    kern = functools.partial(_mixer_prompt_kernel, d=d, cdim=cdim, sdim=sdim)
    tok = lambda i, s: (i, s, 0)
    return pl.pallas_call(
        kern,
        grid=(b, l // tm),
        in_specs=[
            pl.BlockSpec((1, tm, d), tok),
            pl.BlockSpec((1, 1, mod.shape[2]), lambda i, s: (i, 0, 0)),
            _const_spec(g_pre.shape), _const_spec(w_in.shape), _const_spec(cw.shape),
            _const_spec(gs.shape), _const_spec(ws.shape), _const_spec(bias.shape),
            _const_spec(w_out.shape), _const_spec(g_post.shape), _const_spec(g_ffn.shape),
        ],
        out_specs=[
            pl.BlockSpec((1, tm, d), tok),
            pl.BlockSpec((1, tm, d), tok),
            pl.BlockSpec((1, CONV_W - 1, cdim), lambda i, s: (i, 0, 0)),
        ],
        out_shape=[
            jax.ShapeDtypeStruct((b, l, d), jnp.float32),
            jax.ShapeDtypeStruct((b, l, d), MXU_DTYPE),
            jax.ShapeDtypeStruct((b, CONV_W - 1, cdim), jnp.float32),
        ],
        scratch_shapes=[pltpu.VMEM((SUBLANES, cdim), jnp.float32)],
        compiler_params=pltpu.CompilerParams(
            dimension_semantics=("arbitrary", "arbitrary"), vmem_limit_bytes=VMEM_LIMIT_BYTES),
        name="mixer_prompt",
    )(x, mod, g_pre, w_in, cw, gs, ws, bias, w_out, g_post, g_ffn)


def _mixer_sample(x, mod, st, g_pre, w_in, cw, gs, w00, b0, w_out, g_post, g_ffn):
    n, d = x.shape
    cdim = cw.shape[1]
    sdim = gs.shape[1]
    kern = functools.partial(_mixer_sample_kernel, d=d, cdim=cdim, sdim=sdim)
    args = (x, mod, st, g_pre, w_in, cw, gs, w00, b0, w_out, g_post, g_ffn)
    return pl.pallas_call(
        kern,
        grid=(1,),
        in_specs=[_const_spec(a.shape) for a in args],
        out_specs=[_const_spec((n, d)), _const_spec((n, d)),
                   _const_spec((n, (CONV_W - 1) * cdim)), _const_spec((n, sdim))],
        out_shape=[
            jax.ShapeDtypeStruct((n, d), jnp.float32),
            jax.ShapeDtypeStruct((n, d), MXU_DTYPE),
            jax.ShapeDtypeStruct((n, (CONV_W - 1) * cdim), jnp.float32),
            jax.ShapeDtypeStruct((n, sdim), jnp.float32),
        ],
        compiler_params=pltpu.CompilerParams(vmem_limit_bytes=VMEM_LIMIT_BYTES),
        name="mixer_sample",
    )(*args)


def _candidate_cells():
    k = PEER_TOPK
    idx = np.full((10 * SUBLANES,), 1e9, np.float32)
    idx[0:k] = np.arange(k)
    idx[k:k + 8] = k + np.arange(8)
    for a in range(2, 8):
        nb = k // (a + 1)
        base = 24 + 8 * (a - 2)
        idx[base:base + nb] = a * k + np.arange(nb)
    idx[72:80] = (8 + np.arange(8)) * k
    return np.broadcast_to(idx[:, None], (idx.shape[0], LANES)).copy()


def _topk_ranks(s, exact):
    iota16 = lax.broadcasted_iota(jnp.int32, (PEER_TOPK, s.shape[1]), 0)
    if exact:
        iota = lax.broadcasted_iota(jnp.int32, s.shape, 0).astype(jnp.float32)
    rank = jnp.full(s.shape, float(PEER_TOPK), jnp.float32)
    sv = jnp.zeros((PEER_TOPK, s.shape[1]), jnp.float32)
    work = s
    for a in range(PEER_TOPK):
        mx = jnp.max(work, axis=0, keepdims=True)
        if exact:
            idx = jnp.min(jnp.where(work == mx, iota, float(N_KEYS)), axis=0, keepdims=True)
            hit = iota == idx
        else:
            hit = work == mx
        rank = jnp.where(hit, float(a), rank)
        work = jnp.where(hit, -jnp.inf, work)
        sv = jnp.where(iota16 == a, mx, sv)
    return rank, sv


def _count(mask):
    return jnp.sum(jnp.where(mask, 1.0, 0.0), axis=0, keepdims=True)


def _route_head(s1, s2, cidx, exact):
    k = float(PEER_TOPK)
    rank1, sv1 = _topk_ranks(s1, exact)
    rank2, sv2 = _topk_ranks(s2, exact)
    pieces = [sv2 + sv1[0:1], sv2[0:8] + sv1[1:2]]
    for a in range(2, 8):
        pieces.append(sv2[0:8] + sv1[a:a + 1])
    pieces.append(sv1[8:16] + sv2[0:1])
    cand = jnp.concatenate(pieces, axis=0)
    cand = jnp.where(cidx < 1e8, cand, -jnp.inf)
    sel = jnp.zeros(cand.shape, jnp.float32)
    work = cand
    for _ in range(PEER_TOPK):
        mx = jnp.max(work, axis=0, keepdims=True)
        if exact:
            idx = jnp.min(jnp.where(work == mx, cidx, 1e9), axis=0, keepdims=True)
            hit = cidx == idx
        else:
            hit = work == mx
        sel = jnp.where(hit, 1.0, sel)
        work = jnp.where(hit, -jnp.inf, work)
    m = cand[0:1]
    z = jnp.sum(sel * jnp.exp(cand - m), axis=0, keepdims=True)
    n1 = jnp.zeros(s1.shape, jnp.float32)
    n1 = jnp.where(rank1 == 0.0, jnp.sum(sel[0:16], axis=0, keepdims=True), n1)
    for a in range(1, 8):
        base = 16 + 8 * (a - 1)
        n1 = jnp.where(rank1 == float(a), jnp.sum(sel[base:base + 8], axis=0, keepdims=True), n1)
    for a in range(8, 16):
        n1 = jnp.where(rank1 == float(a), sel[72 + a - 8:73 + a - 8], n1)
    a1 = jnp.exp(s1 - sv1[0:1]) / z
    b2 = jnp.exp(s2 - sv2[0:1])
    tied = (jnp.abs(_count(rank1 < k) - k) + jnp.abs(_count(rank2 < k) - k)
            + jnp.abs(jnp.sum(sel, axis=0, keepdims=True) - k))
    return n1, a1, rank2, b2, tied


def _peer_kernel(h2_ref, x1_ref, gt2_ref, wqt_ref, keys_ref, cidx_ref, g_post_ref, u_ref, vt_ref,
                 y_ref,
                 qt_s, n1_s, a1_s, r2_s, b2_s, a_s, w_s, acc_s):
    j = pl.program_id(1)
    tb = h2_ref.shape[0]
    ec = u_ref.shape[0]
    rows_per_step = ec // N_KEYS
    n_sub = tb // LANES
    gdt = r2_s.dtype
    pack = SUBLANES * (4 // jnp.dtype(gdt).itemsize)
    mxu_cols = min(tb, 2 * LANES)
    mxu_rows = ec // 2

    @pl.when(j == 0)
    def _route():
        qt = lax.dot_general(wqt_ref[...], h2_ref[...], _NT, preferred_element_type=jnp.float32)
        qt_s[:, :tb] = _bf16(qt)
        acc_s[...] = jnp.zeros_like(acc_s)

        def sub_body(t, carry):
            lanes = pl.ds(pl.multiple_of(t * LANES, LANES), LANES)
            cidx = cidx_ref[...]

            def route(exact):
                tied = jnp.zeros((1, LANES), jnp.float32)
                for h in range(PEER_HEADS):
                    s = []
                    for p in range(2):
                        hp = 2 * h + p
                        s.append(jnp.dot(keys_ref[hp], qt_s[hp * N_KEYS:(hp + 1) * N_KEYS, lanes],
                                         preferred_element_type=jnp.float32))
                    n1, a1, r2, b2, tied_h = _route_head(s[0], s[1], cidx, exact)
                    n1_s[h, :, lanes] = n1
                    a1_s[h, :, lanes] = a1
                    r2_s[h, :, lanes] = r2.astype(gdt)
                    b2_s[h, :, lanes] = b2.astype(gdt)
                    tied = tied + tied_h
                return tied

            tied = route(exact=False)

            @pl.when(jnp.max(tied) > 0.0)
            def _redo():
                route(exact=True)
            return carry

        lax.fori_loop(0, n_sub, sub_body, 0)

    pieces = [(s0, e0) for s0 in range(0, tb, mxu_cols) for e0 in range(0, ec, mxu_rows)]
    for s0, e0 in pieces:
        cols = slice(s0, s0 + mxu_cols)
        a_s[e0:e0 + mxu_rows, cols] = lax.dot_general(
            u_ref[e0:e0 + mxu_rows, :], h2_ref[cols, :], _NT, preferred_element_type=jnp.float32)

    c0 = pl.multiple_of(j * rows_per_step, SUBLANES)
    for s0, e0 in pieces:
        for t in range(s0 // LANES, (s0 + mxu_cols) // LANES):
            lanes = slice(t * LANES, (t + 1) * LANES)
            for r in range(e0 // N_KEYS, (e0 + mxu_rows) // N_KEYS):
                slab = pl.ds(c0 + (r // SUBLANES) * SUBLANES, SUBLANES)
                rr = r % SUBLANES
                gate = [jnp.zeros((pack, LANES), gdt) for _ in range(N_KEYS // pack)]
                for h in range(PEER_HEADS):
                    n1 = jnp.broadcast_to(n1_s[h, slab, lanes][rr:rr + 1], (pack, LANES)).astype(gdt)
                    a1 = jnp.broadcast_to(a1_s[h, slab, lanes][rr:rr + 1], (pack, LANES)).astype(gdt)
                    for kk in range(N_KEYS // pack):
                        i2 = slice(kk * pack, (kk + 1) * pack)
                        hit = r2_s[h, i2, lanes] < n1
                        gate[kk] = gate[kk] + jnp.where(hit, b2_s[h, i2, lanes], 0.0).astype(gdt) * a1
                for kk in range(N_KEYS // pack):
                    rows = slice(r * N_KEYS + kk * pack, r * N_KEYS + (kk + 1) * pack)
                    a = a_s[rows, lanes]
                    act = 0.5 * a * (1.0 + lax.erf(a * np.float32(np.sqrt(0.5))))
                    w_s[rows, lanes] = (act.astype(gdt) * gate[kk]).astype(w_s.dtype)
        cols = slice(s0, s0 + mxu_cols)
        acc_s[:, cols] += jnp.dot(vt_ref[:, e0:e0 + mxu_rows], w_s[e0:e0 + mxu_rows, cols],
                                  preferred_element_type=jnp.float32)

    @pl.when(j == pl.num_programs(1) - 1)
    def _finish():
        ffn = acc_s[...].T
        y_ref[...] = x1_ref[...] + gt2_ref[0] * _rms(ffn, g_post_ref[...])


def _peer(h2, x1, mod, wqt, keys, cidx, g_post, u_bf, vt_bf, *, tb, blocks_per_mod, ec):
    t, d = h2.shape
    ne = u_bf.shape[0]
    mod_rows = mod.shape[1]
    n_gt2 = 5
    tbp = tb + LANES if (tb // LANES) % 2 == 0 else tb
    return pl.pallas_call(
        _peer_kernel,
        grid=(t // tb, ne // ec),
        in_specs=[
            pl.BlockSpec((tb, d), lambda i, j: (i, 0)),
            pl.BlockSpec((tb, d), lambda i, j: (i, 0)),
            pl.BlockSpec((1, mod_rows, d), lambda i, j: (i // blocks_per_mod, 0, n_gt2)),
            _const_spec(wqt.shape), _const_spec(keys.shape), _const_spec(cidx.shape),
            _const_spec(g_post.shape),
            pl.BlockSpec((ec, d), lambda i, j: (j, 0)),
            pl.BlockSpec((d, ec), lambda i, j: (0, j)),
        ],
        out_specs=pl.BlockSpec((tb, d), lambda i, j: (i, 0)),
        out_shape=jax.ShapeDtypeStruct((t, d), jnp.float32),
        scratch_shapes=[
            pltpu.VMEM((wqt.shape[0], tbp), MXU_DTYPE),
            pltpu.VMEM((PEER_HEADS, N_KEYS, tbp), jnp.float32),
            pltpu.VMEM((PEER_HEADS, N_KEYS, tbp), jnp.float32),
            pltpu.VMEM((PEER_HEADS, N_KEYS, tbp), GATE_DTYPE),
            pltpu.VMEM((PEER_HEADS, N_KEYS, tbp), GATE_DTYPE),
            pltpu.VMEM((ec, tbp), jnp.float32),
            pltpu.VMEM((ec, tbp), MXU_DTYPE),
            pltpu.VMEM((d, tb), jnp.float32),
        ],
        compiler_params=pltpu.CompilerParams(
            dimension_semantics=("arbitrary", "arbitrary"), vmem_limit_bytes=VMEM_LIMIT_BYTES),
        name="peer",
    )(h2, x1, mod, wqt, keys, cidx, g_post, u_bf, vt_bf)


def kernel(x_prompt, x_sample, state_conv, c_prompt, c_sample, w_ada, b_ada, g_pre_mix, w_in, conv_w,
           sgu_norm_g, w_s, b_s, w_out, g_post_mix, g_pre_ffn, w_q, sub_keys, expert_u, expert_v,
           g_post_ffn):
    depth = w_ada.shape[0]
    assert depth == 1, "single-layer step"
    b, l, d = x_prompt.shape
    nb = x_sample.shape[0]
    assert x_sample.shape[1] == 1 and l % CHUNK == 0
    cdim = conv_w.shape[1]
    sdim = sgu_norm_g.shape[1]
    gdim = sdim // SGU_GROUPS
    assert gdim == LANES and sub_keys.shape[3] == N_KEYS and sub_keys.shape[1] == PEER_HEADS
    ne = expert_u.shape[1]
    assert ne == N_KEYS * N_KEYS

    row = lambda a: a.reshape(1, -1)
    w_ada_b = _bf16(w_ada[0])
    w_in_b = _bf16(w_in[0])
    w_out_b = _bf16(w_out[0])
    wqt_b = _bf16(w_q[0].T)
    keys_b = _bf16(sub_keys[0].reshape(2 * PEER_HEADS, N_KEYS, -1))
    u_b = _bf16(expert_u[0])
    vt_b = _bf16(expert_v[0].T)
    cw = conv_w[0].T
    bias = jnp.repeat(b_s[0].T, gdim, axis=1)
    w00 = jnp.repeat(w_s[0][:, 0, 0], gdim).reshape(1, sdim)
    b0 = bias[0:1]
    cidx = jnp.asarray(_candidate_cells())

    c_all = jnp.concatenate([c_prompt, c_sample], axis=0)
    mod = _ada(c_all, w_ada_b, row(b_ada[0]))
    mod_p = mod[:b].reshape(b, 1, -1)
    mod_s = mod[b:]

    tm = min(512, l)
    x1p, h2p, conv_p = _mixer_prompt(
        x_prompt, mod_p, row(g_pre_mix[0]), w_in_b, cw, row(sgu_norm_g[0]), w_s[0], bias, w_out_b,
        row(g_post_mix[0]), row(g_pre_ffn[0]), tm=tm)
    x1s, h2s, conv_s, vn_s = _mixer_sample(
        x_sample.reshape(nb, d), mod_s, state_conv[0].reshape(nb, -1), row(g_pre_mix[0]), w_in_b, cw,
        row(sgu_norm_g[0]), w00, b0, w_out_b, row(g_post_mix[0]), row(g_pre_ffn[0]))

    tb = min(512, l)
    ec = 1024
    yp = _peer(h2p.reshape(b * l, d), x1p.reshape(b * l, d), mod_p, wqt_b, keys_b, cidx,
               row(g_post_ffn[0]), u_b, vt_b, tb=tb, blocks_per_mod=l // tb, ec=ec)
    ys = _peer(h2s, x1s, mod_s.reshape(1, nb, -1), wqt_b, keys_b, cidx,
               row(g_post_ffn[0]), u_b, vt_b, tb=nb, blocks_per_mod=1, ec=ec)

    return (yp.reshape(b, l, d), ys.reshape(nb, 1, d),
            conv_p.reshape(1, b, CONV_W - 1, cdim), conv_s.reshape(1, nb, CONV_W - 1, cdim),
            vn_s.reshape(1, nb, 1, sdim))
```

```python
import functools

import numpy as np
import jax
import jax.numpy as jnp
from jax import lax
from jax.experimental import pallas as pl
from jax.experimental.pallas import tpu as pltpu

EPS = 1e-6
CONV_W = 3
SGU_GROUPS = 4
CHUNK = 128
PEER_HEADS = 8
N_KEYS = 128
PEER_TOPK = 16
LANES = 128
SUBLANES = 8
RANK_MARK = 2.0 ** 20
VMEM_LIMIT_BYTES = 56 * 1024 * 1024

_NT = (((1,), (1,)), ((), ()))


def _rms(x, g):
    return x * lax.rsqrt(jnp.mean(x * x, axis=-1, keepdims=True) + EPS) * g


MXU_DTYPE = jnp.bfloat16
GATE_DTYPE = jnp.float32


def _bf16(x):
    return x.astype(MXU_DTYPE)


def _ada_kernel(c_ref, w_ref, b_ref, o_ref):
    c = c_ref[...]
    s = c * (1.0 / (1.0 + jnp.exp(-c)))
    o_ref[...] = jnp.dot(_bf16(s), w_ref[...], preferred_element_type=jnp.float32) + b_ref[...]


def _ada(c, w_ada, b_ada):
    n, d = c.shape
    e = w_ada.shape[1]
    tn = d
    return pl.pallas_call(
        _ada_kernel,
        grid=(e // tn,),
        in_specs=[
            pl.BlockSpec((n, d), lambda j: (0, 0)),
            pl.BlockSpec((d, tn), lambda j: (0, j)),
            pl.BlockSpec((1, tn), lambda j: (0, j)),
        ],
        out_specs=pl.BlockSpec((n, tn), lambda j: (0, j)),
        out_shape=jax.ShapeDtypeStruct((n, e), jnp.float32),
        name="ada",
    )(c, w_ada, b_ada)


def _group_norm(v, gs, gdim):
    outs = []
    for g in range(SGU_GROUPS):
        vg = v[:, g * gdim:(g + 1) * gdim]
        mu = jnp.mean(vg, axis=-1, keepdims=True)
        d = vg - mu
        yn = d * lax.rsqrt(jnp.mean(d * d, axis=-1, keepdims=True) + EPS)
        outs.append(yn * gs[:, g * gdim:(g + 1) * gdim])
    return jnp.concatenate(outs, axis=-1)


def _mixer_tail(x, yc, ys, mod, d, w_out_ref, g_post_ref, g_ffn_ref, x1_ref, h2_ref):
    gt1 = mod[:, 2 * d:3 * d]
    sh2 = mod[:, 3 * d:4 * d]
    sc2 = mod[:, 4 * d:5 * d]
    mix_in = _bf16(jnp.concatenate([yc, ys], axis=-1))
    mix = jnp.dot(mix_in, w_out_ref[...], preferred_element_type=jnp.float32)
    x1 = x + gt1 * _rms(mix, g_post_ref[...])
    h2 = _rms(x1, g_ffn_ref[...]) * (1.0 + sc2) + sh2
    return x1, h2


def _mixer_prompt_kernel(x_ref, mod_ref, g_pre_ref, w_in_ref, cw_ref, gs_ref, ws_ref, bias_ref,
                         w_out_ref, g_post_ref, g_ffn_ref,
                         x1_ref, h2_ref, conv_ref, carry_ref, *, d, cdim, sdim):
    tm = x_ref.shape[1]
    gdim = sdim // SGU_GROUPS

    @pl.when(pl.program_id(1) == 0)
    def _():
        carry_ref[...] = jnp.zeros_like(carry_ref)

    x = x_ref[0]
    mod = mod_ref[0]
    sh1 = mod[:, 0:d]
    sc1 = mod[:, d:2 * d]
    h = _rms(x, g_pre_ref[...]) * (1.0 + sc1) + sh1
    p = jnp.dot(_bf16(h), w_in_ref[...], preferred_element_type=jnp.float32)
    b_gate = p[:, 0:cdim]
    z = p[:, cdim:2 * cdim] * p[:, 2 * cdim:3 * cdim]
    u = p[:, 3 * cdim:3 * cdim + sdim]
    v = p[:, 3 * cdim + sdim:]

    zext = jnp.concatenate([carry_ref[...], z], axis=0)
    z1 = pltpu.roll(zext, 1, 0)[SUBLANES:]
    z2 = pltpu.roll(zext, 2, 0)[SUBLANES:]
    cw = cw_ref[...]
    y = z2 * cw[0:1] + z1 * cw[1:2] + z * cw[2:3]
    yc = b_gate * y
    carry_ref[...] = z[tm - SUBLANES:]
    conv_ref[0] = z[tm - (CONV_W - 1):]

    vn = _group_norm(v, gs_ref[...], gdim)
    vnb = _bf16(vn)
    row = lax.broadcasted_iota(jnp.int32, (CHUNK, CHUNK), 0)
    col = lax.broadcasted_iota(jnp.int32, (CHUNK, CHUNK), 1)
    bias = bias_ref[...]
    chunks = []
    for ch in range(tm // CHUNK):
        groups = []
        for g in range(SGU_GROUPS):
            wt = _bf16(jnp.where(row >= col, ws_ref[g], 0.0))
            groups.append(jnp.dot(wt, vnb[ch * CHUNK:(ch + 1) * CHUNK, g * gdim:(g + 1) * gdim],
                                  preferred_element_type=jnp.float32))
        chunks.append(jnp.concatenate(groups, axis=-1) + bias)
    mixed = jnp.concatenate(chunks, axis=0)
    ys = u * mixed

    x1, h2 = _mixer_tail(x, yc, ys, mod, d, w_out_ref, g_post_ref, g_ffn_ref, x1_ref, h2_ref)
    x1_ref[0] = x1
    h2_ref[0] = _bf16(h2)


def _mixer_sample_kernel(x_ref, mod_ref, st_ref, g_pre_ref, w_in_ref, cw_ref, gs_ref, w00_ref, b0_ref,
                         w_out_ref, g_post_ref, g_ffn_ref,
                         x1_ref, h2_ref, conv_ref, vn_ref, *, d, cdim, sdim):
    gdim = sdim // SGU_GROUPS
    x = x_ref[...]
    mod = mod_ref[...]
    sh1 = mod[:, 0:d]
    sc1 = mod[:, d:2 * d]
    h = _rms(x, g_pre_ref[...]) * (1.0 + sc1) + sh1
    p = jnp.dot(_bf16(h), w_in_ref[...], preferred_element_type=jnp.float32)
    b_gate = p[:, 0:cdim]
    z = p[:, cdim:2 * cdim] * p[:, 2 * cdim:3 * cdim]
    u = p[:, 3 * cdim:3 * cdim + sdim]
    v = p[:, 3 * cdim + sdim:]

    st = st_ref[...]
    z2 = st[:, 0:cdim]
    z1 = st[:, cdim:]
    cw = cw_ref[...]
    y = z2 * cw[0:1] + z1 * cw[1:2] + z * cw[2:3]
    yc = b_gate * y
    conv_ref[...] = jnp.concatenate([z1, z], axis=-1)

    vn = _group_norm(v, gs_ref[...], gdim)
    vn_ref[...] = vn
    mixed = vn * w00_ref[...] + b0_ref[...]
    ys = u * mixed

    x1, h2 = _mixer_tail(x, yc, ys, mod, d, w_out_ref, g_post_ref, g_ffn_ref, x1_ref, h2_ref)
    x1_ref[...] = x1
    h2_ref[...] = _bf16(h2)


def _const_spec(shape):
    zeros = (0,) * len(shape)
    return pl.BlockSpec(shape, lambda *_: zeros)


def _mixer_prompt(x, mod, g_pre, w_in, cw, gs, ws, bias, w_out, g_post, g_ffn, *, tm):
    b, l, d = x.shape
    cdim = cw.shape[1]
    sdim = gs.shape[1]
    kern = functools.partial(_mixer_prompt_kernel, d=d, cdim=cdim, sdim=sdim)
    tok = lambda i, s: (i, s, 0)
    return pl.pallas_call(
        kern,
        grid=(b, l // tm),
        in_specs=[
            pl.BlockSpec((1, tm, d), tok),
            pl.BlockSpec((1, 1, mod.shape[2]), lambda i, s: (i, 0, 0)),
            _const_spec(g_pre.shape), _const_spec(w_in.shape), _const_spec(cw.shape),
            _const_spec(gs.shape), _const_spec(ws.shape), _const_spec(bias.shape),
            _const_spec(w_out.shape), _const_spec(g_post.shape), _const_spec(g_ffn.shape),
        ],
        out_specs=[
            pl.BlockSpec((1, tm, d), tok),
            pl.BlockSpec((1, tm, d), tok),
            pl.BlockSpec((1, CONV_W - 1, cdim), lambda i, s: (i, 0, 0)),
        ],
        out_shape=[
            jax.ShapeDtypeStruct((b, l, d), jnp.float32),
            jax.ShapeDtypeStruct((b, l, d), MXU_DTYPE),
            jax.ShapeDtypeStruct((b, CONV_W - 1, cdim), jnp.float32),
        ],
        scratch_shapes=[pltpu.VMEM((SUBLANES, cdim), jnp.float32)],
        compiler_params=pltpu.CompilerParams(
            dimension_semantics=("arbitrary", "arbitrary"), vmem_limit_bytes=VMEM_LIMIT_BYTES),
        name="mixer_prompt",
    )(x, mod, g_pre, w_in, cw, gs, ws, bias, w_out, g_post, g_ffn)


def _mixer_sample(x, mod, st, g_pre, w_in, cw, gs, w00, b0, w_out, g_post, g_ffn):
    n, d = x.shape
    cdim = cw.shape[1]
    sdim = gs.shape[1]
    kern = functools.partial(_mixer_sample_kernel, d=d, cdim=cdim, sdim=sdim)
    args = (x, mod, st, g_pre, w_in, cw, gs, w00, b0, w_out, g_post, g_ffn)
    return pl.pallas_call(
        kern,
        grid=(1,),
        in_specs=[_const_spec(a.shape) for a in args],
        out_specs=[_const_spec((n, d)), _const_spec((n, d)),
                   _const_spec((n, (CONV_W - 1) * cdim)), _const_spec((n, sdim))],
        out_shape=[
            jax.ShapeDtypeStruct((n, d), jnp.float32),
            jax.ShapeDtypeStruct((n, d), MXU_DTYPE),
            jax.ShapeDtypeStruct((n, (CONV_W - 1) * cdim), jnp.float32),
            jax.ShapeDtypeStruct((n, sdim), jnp.float32),
        ],
        compiler_params=pltpu.CompilerParams(vmem_limit_bytes=VMEM_LIMIT_BYTES),
        name="mixer_sample",
    )(*args)


def _candidate_cells():
    k = PEER_TOPK
    idx = np.full((10 * SUBLANES,), 1e9, np.float32)
    idx[0:k] = np.arange(k)
    idx[k:k + 8] = k + np.arange(8)
    for a in range(2, 8):
        nb = k // (a + 1)
        base = 24 + 8 * (a - 2)
        idx[base:base + nb] = a * k + np.arange(nb)
    idx[72:80] = (8 + np.arange(8)) * k
    return np.broadcast_to(idx[:, None], (idx.shape[0], LANES)).copy()


def _rank_code(a):
    return -(a + 1) * RANK_MARK


def _topk_ranks(s, exact):
    iota16 = lax.broadcasted_iota(jnp.int32, (PEER_TOPK, s.shape[1]), 0)
    sv = jnp.zeros((PEER_TOPK, s.shape[1]), jnp.float32)
    work = s
    if exact:
        iota = lax.broadcasted_iota(jnp.int32, s.shape, 0).astype(jnp.float32)
        code = jnp.full(s.shape, -jnp.inf, jnp.float32)
        for a in range(PEER_TOPK):
            mx = jnp.max(work, axis=0, keepdims=True)
            idx = jnp.min(jnp.where(work == mx, iota, float(N_KEYS)), axis=0, keepdims=True)
            hit = iota == idx
            code = jnp.where(hit, _rank_code(a), code)
            work = jnp.where(hit, -jnp.inf, work)
            sv = jnp.where(iota16 == a, mx, sv)
        return code, sv
    for a in range(PEER_TOPK):
        mx = jnp.max(work, axis=0, keepdims=True)
        work = jnp.where(work == mx, _rank_code(a), work)
        sv = jnp.where(iota16 == a, mx, sv)
    return jnp.where(work <= -RANK_MARK, work, -jnp.inf), sv


def _count(mask):
    return jnp.sum(jnp.where(mask, 1.0, 0.0), axis=0, keepdims=True)


def _route_head(s1, s2, cidx, exact):
    k = float(PEER_TOPK)
    code1, sv1 = _topk_ranks(s1, exact)
    code2, sv2 = _topk_ranks(s2, exact)
    pieces = [sv2 + sv1[0:1], sv2[0:8] + sv1[1:2]]
    for a in range(2, 8):
        pieces.append(sv2[0:8] + sv1[a:a + 1])
    pieces.append(sv1[8:16] + sv2[0:1])
    cand = jnp.concatenate(pieces, axis=0)
    cand = jnp.where(cidx < 1e8, cand, -jnp.inf)
    sel = jnp.zeros(cand.shape, jnp.float32)
    work = cand
    for _ in range(PEER_TOPK):
        mx = jnp.max(work, axis=0, keepdims=True)
        idx = jnp.min(jnp.where(work == mx, cidx, 1e9), axis=0, keepdims=True)
        hit = cidx == idx
        sel = jnp.where(hit, 1.0, sel)
        work = jnp.where(hit, -jnp.inf, work)
    m = cand[0:1]
    z = jnp.sum(sel * jnp.exp(cand - m), axis=0, keepdims=True)
    thr = sel * -RANK_MARK
    n1 = jnp.zeros(s1.shape, jnp.float32)
    n1 = jnp.where(code1 == _rank_code(0), jnp.sum(thr[0:16], axis=0, keepdims=True), n1)
    for a in range(1, 8):
        base = 16 + 8 * (a - 1)
        n1 = jnp.where(code1 == _rank_code(a), jnp.sum(thr[base:base + 8], axis=0, keepdims=True), n1)
    for a in range(8, 16):
        n1 = jnp.where(code1 == _rank_code(a), thr[72 + a - 8:73 + a - 8], n1)
    a1 = 0.5 * jnp.exp(s1 - sv1[0:1]) / z
    b2 = jnp.exp(s2 - sv2[0:1])
    tied = jnp.abs(_count(code1 > -jnp.inf) - k) + jnp.abs(_count(code2 > -jnp.inf) - k)
    return n1, a1, code2, b2, tied


def _peer_kernel(h2_ref, x1_ref, gt2_ref, wqt_ref, keys_ref, cidx_ref, g_post_ref, u_ref, vt_ref,
                 y_ref,
                 qt_s, n1_s, a1_s, r2_s, b2_s, *piece_s):
    j = pl.program_id(1)
    tb = h2_ref.shape[0]
    ec = u_ref.shape[0]
    rows_per_step = ec // N_KEYS
    n_sub = tb // LANES
    gdt = r2_s.dtype
    pack = SUBLANES * (4 // jnp.dtype(gdt).itemsize)
    mxu_cols, mxu_rows = _piece_shape(tb, ec)
    pieces = [(s0, e0) for s0 in range(0, tb, mxu_cols) for e0 in range(0, ec, mxu_rows)]
    a_p = piece_s[:len(pieces)]
    w_p = piece_s[len(pieces):2 * len(pieces)]
    acc_p = piece_s[2 * len(pieces):]

    @pl.when(j == 0)
    def _route():
        qt = lax.dot_general(wqt_ref[...], h2_ref[...], _NT, preferred_element_type=jnp.float32)
        qt_s[:, :tb] = _bf16(qt)
        for acc in acc_p:
            acc[...] = jnp.zeros_like(acc)

        def sub_body(t, carry):
            lanes = pl.ds(pl.multiple_of(t * LANES, LANES), LANES)
            cidx = cidx_ref[...]

            def route(exact):
                tied = jnp.zeros((1, LANES), jnp.float32)
                for h in range(PEER_HEADS):
                    s = []
                    for p in range(2):
                        hp = 2 * h + p
                        s.append(jnp.dot(keys_ref[hp], qt_s[hp * N_KEYS:(hp + 1) * N_KEYS, lanes],
                                         preferred_element_type=jnp.float32))
                    n1, a1, r2, b2, tied_h = _route_head(s[0], s[1], cidx, exact)
                    n1_s[h, :, lanes] = n1
                    a1_s[h, :, lanes] = a1
                    r2_s[h, :, lanes] = r2.astype(gdt)
                    b2_s[h, :, lanes] = b2.astype(gdt)
                    tied = tied + tied_h
                return tied

            tied = route(exact=False)

            @pl.when(jnp.max(tied) > 0.0)
            def _redo():
                route(exact=True)
            return carry

        lax.fori_loop(0, n_sub, sub_body, 0)

    c0 = pl.multiple_of(j * rows_per_step, SUBLANES)

    def mm1(k):
        s0, e0 = pieces[k]
        a_p[k][:, :mxu_cols] = lax.dot_general(
            u_ref[e0:e0 + mxu_rows, :], h2_ref[s0:s0 + mxu_cols, :], _NT,
            preferred_element_type=jnp.float32)

    def gate_piece(k):
        s0, e0 = pieces[k]
        for t in range(mxu_cols // LANES):
            lanes = slice(s0 + t * LANES, s0 + (t + 1) * LANES)
            plane = slice(t * LANES, (t + 1) * LANES)
            for r in range(mxu_rows // N_KEYS):
                c = e0 // N_KEYS + r
                slab = pl.ds(c0 + (c // SUBLANES) * SUBLANES, SUBLANES)
                rr = c % SUBLANES
                gate = [jnp.zeros((pack, LANES), gdt) for _ in range(N_KEYS // pack)]
                for h in range(PEER_HEADS):
                    n1 = jnp.broadcast_to(n1_s[h, slab, lanes][rr:rr + 1], (pack, LANES)).astype(gdt)
                    a1 = jnp.broadcast_to(a1_s[h, slab, lanes][rr:rr + 1], (pack, LANES)).astype(gdt)
                    for kk in range(N_KEYS // pack):
                        i2 = slice(kk * pack, (kk + 1) * pack)
                        hit = r2_s[h, i2, lanes] >= n1
                        gate[kk] = gate[kk] + jnp.where(hit, b2_s[h, i2, lanes], 0.0).astype(gdt) * a1
                for kk in range(N_KEYS // pack):
                    rows = slice(r * N_KEYS + kk * pack, r * N_KEYS + (kk + 1) * pack)
                    a = a_p[k][rows, plane]
                    act = a * (1.0 + lax.erf(a * np.float32(np.sqrt(0.5))))
                    w_p[k][rows, plane] = (act.astype(gdt) * gate[kk]).astype(w_p[k].dtype)

    def mm2(k):
        s0, e0 = pieces[k]
        acc_p[s0 // mxu_cols][...] += jnp.dot(vt_ref[:, e0:e0 + mxu_rows], w_p[k][:, :mxu_cols],
                                              preferred_element_type=jnp.float32)

    for k in range(len(pieces) + 2):
        if k < len(pieces):
            mm1(k)
        if 1 <= k <= len(pieces):
            gate_piece(k - 1)
        if k >= 2:
            mm2(k - 2)

    @pl.when(j == pl.num_programs(1) - 1)
    def _finish():
        gt2 = gt2_ref[0]
        for ci, acc in enumerate(acc_p):
            tok = slice(ci * mxu_cols, (ci + 1) * mxu_cols)
            g = gt2 if gt2.shape[0] == 1 else gt2[tok]
            y_ref[tok, :] = x1_ref[tok, :] + g * _rms(acc[...].T, g_post_ref[...])


def _piece_shape(tb, ec):
    return min(tb, 2 * LANES), ec // 2


def _odd_lane_tiles(n):
    tiles = -(-n // LANES)
    return (tiles + 1 - tiles % 2) * LANES


def _peer(h2, x1, mod, wqt, keys, cidx, g_post, u_bf, vt_bf, *, tb, blocks_per_mod, ec):
    t, d = h2.shape
    ne = u_bf.shape[0]
    mod_rows = mod.shape[1]
    n_gt2 = 5
    tbp = _odd_lane_tiles(tb)
    mxu_cols, mxu_rows = _piece_shape(tb, ec)
    n_col = tb // mxu_cols
    n_pieces = n_col * (ec // mxu_rows)
    colp = _odd_lane_tiles(mxu_cols)
    return pl.pallas_call(
        _peer_kernel,
        grid=(t // tb, ne // ec),
        in_specs=[
            pl.BlockSpec((tb, d), lambda i, j: (i, 0)),
            pl.BlockSpec((tb, d), lambda i, j: (i, 0)),
            pl.BlockSpec((1, mod_rows, d), lambda i, j: (i // blocks_per_mod, 0, n_gt2)),
            _const_spec(wqt.shape), _const_spec(keys.shape), _const_spec(cidx.shape),
            _const_spec(g_post.shape),
            pl.BlockSpec((ec, d), lambda i, j: (j, 0)),
            pl.BlockSpec((d, ec), lambda i, j: (0, j)),
        ],
        out_specs=pl.BlockSpec((tb, d), lambda i, j: (i, 0)),
        out_shape=jax.ShapeDtypeStruct((t, d), jnp.float32),
        scratch_shapes=[
            pltpu.VMEM((wqt.shape[0], tbp), MXU_DTYPE),
            pltpu.VMEM((PEER_HEADS, N_KEYS, tbp), jnp.float32),
            pltpu.VMEM((PEER_HEADS, N_KEYS, tbp), jnp.float32),
            pltpu.VMEM((PEER_HEADS, N_KEYS, tbp), GATE_DTYPE),
            pltpu.VMEM((PEER_HEADS, N_KEYS, tbp), GATE_DTYPE),
        ] + [pltpu.VMEM((mxu_rows, colp), jnp.float32) for _ in range(n_pieces)]
          + [pltpu.VMEM((mxu_rows, colp), MXU_DTYPE) for _ in range(n_pieces)]
          + [pltpu.VMEM((d, mxu_cols), jnp.float32) for _ in range(n_col)],
        compiler_params=pltpu.CompilerParams(
            dimension_semantics=("arbitrary", "arbitrary"), vmem_limit_bytes=VMEM_LIMIT_BYTES),
        name="peer",
    )(h2, x1, mod, wqt, keys, cidx, g_post, u_bf, vt_bf)


def kernel(x_prompt, x_sample, state_conv, c_prompt, c_sample, w_ada, b_ada, g_pre_mix, w_in, conv_w,
           sgu_norm_g, w_s, b_s, w_out, g_post_mix, g_pre_ffn, w_q, sub_keys, expert_u, expert_v,
           g_post_ffn):
    depth = w_ada.shape[0]
    assert depth == 1, "single-layer step"
    b, l, d = x_prompt.shape
    nb = x_sample.shape[0]
    assert x_sample.shape[1] == 1 and l % CHUNK == 0
    cdim = conv_w.shape[1]
    sdim = sgu_norm_g.shape[1]
    gdim = sdim // SGU_GROUPS
    assert gdim == LANES and sub_keys.shape[3] == N_KEYS and sub_keys.shape[1] == PEER_HEADS
    ne = expert_u.shape[1]
    assert ne == N_KEYS * N_KEYS

    row = lambda a: a.reshape(1, -1)
    w_ada_b = _bf16(w_ada[0])
    w_in_b = _bf16(w_in[0])
    w_out_b = _bf16(w_out[0])
    wqt_b = _bf16(w_q[0].T)
    keys_b = _bf16(sub_keys[0].reshape(2 * PEER_HEADS, N_KEYS, -1))
    u_b = _bf16(expert_u[0])
    vt_b = _bf16(expert_v[0].T)
    cw = conv_w[0].T
    bias = jnp.repeat(b_s[0].T, gdim, axis=1)
    w00 = jnp.repeat(w_s[0][:, 0, 0], gdim).reshape(1, sdim)
    b0 = bias[0:1]
    cidx = jnp.asarray(_candidate_cells())

    c_all = jnp.concatenate([c_prompt, c_sample], axis=0)
    mod = _ada(c_all, w_ada_b, row(b_ada[0]))
    mod_p = mod[:b].reshape(b, 1, -1)
    mod_s = mod[b:]

    tm = min(512, l)
    x1p, h2p, conv_p = _mixer_prompt(
        x_prompt, mod_p, row(g_pre_mix[0]), w_in_b, cw, row(sgu_norm_g[0]), w_s[0], bias, w_out_b,
        row(g_post_mix[0]), row(g_pre_ffn[0]), tm=tm)
    x1s, h2s, conv_s, vn_s = _mixer_sample(
        x_sample.reshape(nb, d), mod_s, state_conv[0].reshape(nb, -1), row(g_pre_mix[0]), w_in_b, cw,
        row(sgu_norm_g[0]), w00, b0, w_out_b, row(g_post_mix[0]), row(g_pre_ffn[0]))

    tb = min(512, l)
    ec = 1024
    yp = _peer(h2p.reshape(b * l, d), x1p.reshape(b * l, d), mod_p, wqt_b, keys_b, cidx,
               row(g_post_ffn[0]), u_b, vt_b, tb=tb, blocks_per_mod=l // tb, ec=ec)
    ys = _peer(h2s, x1s, mod_s.reshape(1, nb, -1), wqt_b, keys_b, cidx,
               row(g_post_ffn[0]), u_b, vt_b, tb=nb, blocks_per_mod=1, ec=ec)

    return (yp.reshape(b, l, d), ys.reshape(nb, 1, d),
            conv_p.reshape(1, b, CONV_W - 1, cdim), conv_s.reshape(1, nb, CONV_W - 1, cdim),
            vn_s.reshape(1, nb, 1, sdim))
```

```python
import functools

import numpy as np
import jax
import jax.numpy as jnp
from jax import lax
from jax.experimental import pallas as pl
from jax.experimental.pallas import tpu as pltpu

EPS = 1e-6
CONV_W = 3
SGU_GROUPS = 4
CHUNK = 128
PEER_HEADS = 8
N_KEYS = 128
PEER_TOPK = 16
LANES = 128
SUBLANES = 8
RANK_MARK = 2.0 ** 20
VMEM_LIMIT_BYTES = 56 * 1024 * 1024

_NT = (((1,), (1,)), ((), ()))


def _rms(x, g):
    return x * lax.rsqrt(jnp.mean(x * x, axis=-1, keepdims=True) + EPS) * g


MXU_DTYPE = jnp.bfloat16
GATE_DTYPE = jnp.float32


def _bf16(x):
    return x.astype(MXU_DTYPE)


def _ada_kernel(c_ref, w_ref, b_ref, o_ref):
    c = c_ref[...]
    s = c * (1.0 / (1.0 + jnp.exp(-c)))
    o_ref[...] = jnp.dot(_bf16(s), w_ref[...], preferred_element_type=jnp.float32) + b_ref[...]


def _ada(c, w_ada, b_ada):
    n, d = c.shape
    e = w_ada.shape[1]
    tn = d
    return pl.pallas_call(
        _ada_kernel,
        grid=(e // tn,),
        in_specs=[
            pl.BlockSpec((n, d), lambda j: (0, 0)),
            pl.BlockSpec((d, tn), lambda j: (0, j)),
            pl.BlockSpec((1, tn), lambda j: (0, j)),
        ],
        out_specs=pl.BlockSpec((n, tn), lambda j: (0, j)),
        out_shape=jax.ShapeDtypeStruct((n, e), jnp.float32),
        name="ada",
    )(c, w_ada, b_ada)


def _group_norm(v, gs, gdim):
    outs = []
    for g in range(SGU_GROUPS):
        vg = v[:, g * gdim:(g + 1) * gdim]
        mu = jnp.mean(vg, axis=-1, keepdims=True)
        d = vg - mu
        yn = d * lax.rsqrt(jnp.mean(d * d, axis=-1, keepdims=True) + EPS)
        outs.append(yn * gs[:, g * gdim:(g + 1) * gdim])
    return jnp.concatenate(outs, axis=-1)


def _mixer_tail(x, yc, ys, mod, d, w_out_ref, g_post_ref, g_ffn_ref, x1_ref, h2_ref):
    gt1 = mod[:, 2 * d:3 * d]
    sh2 = mod[:, 3 * d:4 * d]
    sc2 = mod[:, 4 * d:5 * d]
    mix_in = _bf16(jnp.concatenate([yc, ys], axis=-1))
    mix = jnp.dot(mix_in, w_out_ref[...], preferred_element_type=jnp.float32)
    x1 = x + gt1 * _rms(mix, g_post_ref[...])
    h2 = _rms(x1, g_ffn_ref[...]) * (1.0 + sc2) + sh2
    return x1, h2


def _mixer_prompt_kernel(x_ref, mod_ref, g_pre_ref, w_in_ref, cw_ref, gs_ref, ws_ref, bias_ref,
                         w_out_ref, g_post_ref, g_ffn_ref, eu_ref, ev_ref,
                         x1_ref, h2_ref, conv_ref, ub_ref, vtb_ref, carry_ref, *, d, cdim, sdim):
    tm = x_ref.shape[1]
    gdim = sdim // SGU_GROUPS

    @pl.when(pl.program_id(1) == 0)
    def _():
        carry_ref[...] = jnp.zeros_like(carry_ref)

    x = x_ref[0]
    mod = mod_ref[0]
    sh1 = mod[:, 0:d]
    sc1 = mod[:, d:2 * d]
    h = _rms(x, g_pre_ref[...]) * (1.0 + sc1) + sh1
    p = jnp.dot(_bf16(h), w_in_ref[...], preferred_element_type=jnp.float32)
    b_gate = p[:, 0:cdim]
    z = p[:, cdim:2 * cdim] * p[:, 2 * cdim:3 * cdim]
    u = p[:, 3 * cdim:3 * cdim + sdim]
    v = p[:, 3 * cdim + sdim:]

    zext = jnp.concatenate([carry_ref[...], z], axis=0)
    z1 = pltpu.roll(zext, 1, 0)[SUBLANES:]
    z2 = pltpu.roll(zext, 2, 0)[SUBLANES:]
    cw = cw_ref[...]
    y = z2 * cw[0:1] + z1 * cw[1:2] + z * cw[2:3]
    yc = b_gate * y
    carry_ref[...] = z[tm - SUBLANES:]
    conv_ref[0] = z[tm - (CONV_W - 1):]

    vn = _group_norm(v, gs_ref[...], gdim)
    vnb = _bf16(vn)
    row = lax.broadcasted_iota(jnp.int32, (CHUNK, CHUNK), 0)
    col = lax.broadcasted_iota(jnp.int32, (CHUNK, CHUNK), 1)
    bias = bias_ref[...]
    chunks = []
    for ch in range(tm // CHUNK):
        groups = []
        for g in range(SGU_GROUPS):
            wt = _bf16(jnp.where(row >= col, ws_ref[g], 0.0))
            groups.append(jnp.dot(wt, vnb[ch * CHUNK:(ch + 1) * CHUNK, g * gdim:(g + 1) * gdim],
                                  preferred_element_type=jnp.float32))
        chunks.append(jnp.concatenate(groups, axis=-1) + bias)
    mixed = jnp.concatenate(chunks, axis=0)
    ys = u * mixed

    x1, h2 = _mixer_tail(x, yc, ys, mod, d, w_out_ref, g_post_ref, g_ffn_ref, x1_ref, h2_ref)
    x1_ref[0] = x1
    h2_ref[0] = _bf16(h2)

    ub_ref[...] = _bf16(eu_ref[...])
    vtb_ref[...] = _bf16(ev_ref[...].T)


def _mixer_sample_kernel(x_ref, mod_ref, st_ref, g_pre_ref, w_in_ref, cw_ref, gs_ref, w00_ref, b0_ref,
                         w_out_ref, g_post_ref, g_ffn_ref,
                         x1_ref, h2_ref, conv_ref, vn_ref, *, d, cdim, sdim):
    gdim = sdim // SGU_GROUPS
    x = x_ref[...]
    mod = mod_ref[...]
    sh1 = mod[:, 0:d]
    sc1 = mod[:, d:2 * d]
    h = _rms(x, g_pre_ref[...]) * (1.0 + sc1) + sh1
    p = jnp.dot(_bf16(h), w_in_ref[...], preferred_element_type=jnp.float32)
    b_gate = p[:, 0:cdim]
    z = p[:, cdim:2 * cdim] * p[:, 2 * cdim:3 * cdim]
    u = p[:, 3 * cdim:3 * cdim + sdim]
    v = p[:, 3 * cdim + sdim:]

    st = st_ref[...]
    z2 = st[:, 0:cdim]
    z1 = st[:, cdim:]
    cw = cw_ref[...]
    y = z2 * cw[0:1] + z1 * cw[1:2] + z * cw[2:3]
    yc = b_gate * y
    conv_ref[...] = jnp.concatenate([z1, z], axis=-1)

    vn = _group_norm(v, gs_ref[...], gdim)
    vn_ref[...] = vn
    mixed = vn * w00_ref[...] + b0_ref[...]
    ys = u * mixed

    x1, h2 = _mixer_tail(x, yc, ys, mod, d, w_out_ref, g_post_ref, g_ffn_ref, x1_ref, h2_ref)
    x1_ref[...] = x1
    h2_ref[...] = _bf16(h2)


def _const_spec(shape):
    zeros = (0,) * len(shape)
    return pl.BlockSpec(shape, lambda *_: zeros)


def _mixer_prompt(x, mod, g_pre, w_in, cw, gs, ws, bias, w_out, g_post, g_ffn, eu, ev, *, tm):
    b, l, d = x.shape
    cdim = cw.shape[1]
    sdim = gs.shape[1]
    ne = eu.shape[0]
    steps_per_seq = l // tm
    rows = ne // (b * steps_per_seq)
    assert rows * b * steps_per_seq == ne and rows % LANES == 0
    kern = functools.partial(_mixer_prompt_kernel, d=d, cdim=cdim, sdim=sdim)
    tok = lambda i, s: (i, s, 0)
    return pl.pallas_call(
        kern,
        grid=(b, l // tm),
        in_specs=[
            pl.BlockSpec((1, tm, d), tok),
            pl.BlockSpec((1, 1, mod.shape[2]), lambda i, s: (i, 0, 0)),
            _const_spec(g_pre.shape), _const_spec(w_in.shape), _const_spec(cw.shape),
            _const_spec(gs.shape), _const_spec(ws.shape), _const_spec(bias.shape),
            _const_spec(w_out.shape), _const_spec(g_post.shape), _const_spec(g_ffn.shape),
            pl.BlockSpec((rows, d), lambda i, s: (i * steps_per_seq + s, 0)),
            pl.BlockSpec((rows, d), lambda i, s: (i * steps_per_seq + s, 0)),
        ],
        out_specs=[
            pl.BlockSpec((1, tm, d), tok),
            pl.BlockSpec((1, tm, d), tok),
            pl.BlockSpec((1, CONV_W - 1, cdim), lambda i, s: (i, 0, 0)),
            pl.BlockSpec((rows, d), lambda i, s: (i * steps_per_seq + s, 0)),
            pl.BlockSpec((d, rows), lambda i, s: (0, i * steps_per_seq + s)),
        ],
        out_shape=[
            jax.ShapeDtypeStruct((b, l, d), jnp.float32),
            jax.ShapeDtypeStruct((b, l, d), MXU_DTYPE),
            jax.ShapeDtypeStruct((b, CONV_W - 1, cdim), jnp.float32),
            jax.ShapeDtypeStruct((ne, d), MXU_DTYPE),
            jax.ShapeDtypeStruct((d, ne), MXU_DTYPE),
        ],
        scratch_shapes=[pltpu.VMEM((SUBLANES, cdim), jnp.float32)],
        compiler_params=pltpu.CompilerParams(
            dimension_semantics=("arbitrary", "arbitrary"), vmem_limit_bytes=VMEM_LIMIT_BYTES),
        name="mixer_prompt",
    )(x, mod, g_pre, w_in, cw, gs, ws, bias, w_out, g_post, g_ffn, eu, ev)


def _mixer_sample(x, mod, st, g_pre, w_in, cw, gs, w00, b0, w_out, g_post, g_ffn):
    n, d = x.shape
    cdim = cw.shape[1]
    sdim = gs.shape[1]
    kern = functools.partial(_mixer_sample_kernel, d=d, cdim=cdim, sdim=sdim)
    args = (x, mod, st, g_pre, w_in, cw, gs, w00, b0, w_out, g_post, g_ffn)
    return pl.pallas_call(
        kern,
        grid=(1,),
        in_specs=[_const_spec(a.shape) for a in args],
        out_specs=[_const_spec((n, d)), _const_spec((n, d)),
                   _const_spec((n, (CONV_W - 1) * cdim)), _const_spec((n, sdim))],
        out_shape=[
            jax.ShapeDtypeStruct((n, d), jnp.float32),
            jax.ShapeDtypeStruct((n, d), MXU_DTYPE),
            jax.ShapeDtypeStruct((n, (CONV_W - 1) * cdim), jnp.float32),
            jax.ShapeDtypeStruct((n, sdim), jnp.float32),
        ],
        compiler_params=pltpu.CompilerParams(vmem_limit_bytes=VMEM_LIMIT_BYTES),
        name="mixer_sample",
    )(*args)


def _candidate_cells():
    k = PEER_TOPK
    idx = np.full((10 * SUBLANES,), 1e9, np.float32)
    idx[0:k] = np.arange(k)
    idx[k:k + 8] = k + np.arange(8)
    for a in range(2, 8):
        nb = k // (a + 1)
        base = 24 + 8 * (a - 2)
        idx[base:base + nb] = a * k + np.arange(nb)
    idx[72:80] = (8 + np.arange(8)) * k
    return np.broadcast_to(idx[:, None], (idx.shape[0], LANES)).copy()


def _rank_code(a):
    return -(a + 1) * RANK_MARK


def _topk_ranks(s, exact):
    iota16 = lax.broadcasted_iota(jnp.int32, (PEER_TOPK, s.shape[1]), 0)
    sv = jnp.zeros((PEER_TOPK, s.shape[1]), jnp.float32)
    work = s
    if exact:
        iota = lax.broadcasted_iota(jnp.int32, s.shape, 0).astype(jnp.float32)
        code = jnp.full(s.shape, -jnp.inf, jnp.float32)
        for a in range(PEER_TOPK):
            mx = jnp.max(work, axis=0, keepdims=True)
            idx = jnp.min(jnp.where(work == mx, iota, float(N_KEYS)), axis=0, keepdims=True)
            hit = iota == idx
            code = jnp.where(hit, _rank_code(a), code)
            work = jnp.where(hit, -jnp.inf, work)
            sv = jnp.where(iota16 == a, mx, sv)
        return code, sv
    for a in range(PEER_TOPK):
        mx = jnp.max(work, axis=0, keepdims=True)
        work = jnp.where(work == mx, _rank_code(a), work)
        sv = jnp.where(iota16 == a, mx, sv)
    return jnp.where(work <= -RANK_MARK, work, -jnp.inf), sv


def _count(mask):
    return jnp.sum(jnp.where(mask, 1.0, 0.0), axis=0, keepdims=True)


def _route_head(s1, s2, cidx, exact):
    k = float(PEER_TOPK)
    code1, sv1 = _topk_ranks(s1, exact)
    code2, sv2 = _topk_ranks(s2, exact)
    pieces = [sv2 + sv1[0:1], sv2[0:8] + sv1[1:2]]
    for a in range(2, 8):
        pieces.append(sv2[0:8] + sv1[a:a + 1])
    pieces.append(sv1[8:16] + sv2[0:1])
    cand = jnp.concatenate(pieces, axis=0)
    cand = jnp.where(cidx < 1e8, cand, -jnp.inf)
    sel = jnp.zeros(cand.shape, jnp.float32)
    work = cand
    for _ in range(PEER_TOPK):
        mx = jnp.max(work, axis=0, keepdims=True)
        idx = jnp.min(jnp.where(work == mx, cidx, 1e9), axis=0, keepdims=True)
        hit = cidx == idx
        sel = jnp.where(hit, 1.0, sel)
        work = jnp.where(hit, -jnp.inf, work)
    m = cand[0:1]
    z = jnp.sum(sel * jnp.exp(cand - m), axis=0, keepdims=True)
    thr = sel * -RANK_MARK
    n1 = jnp.zeros(s1.shape, jnp.float32)
    n1 = jnp.where(code1 == _rank_code(0), jnp.sum(thr[0:16], axis=0, keepdims=True), n1)
    for a in range(1, 8):
        base = 16 + 8 * (a - 1)
        n1 = jnp.where(code1 == _rank_code(a), jnp.sum(thr[base:base + 8], axis=0, keepdims=True), n1)
    for a in range(8, 16):
        n1 = jnp.where(code1 == _rank_code(a), thr[72 + a - 8:73 + a - 8], n1)
    a1 = 0.5 * jnp.exp(s1 - sv1[0:1]) / z
    b2 = jnp.exp(s2 - sv2[0:1])
    tied = jnp.abs(_count(code1 > -jnp.inf) - k) + jnp.abs(_count(code2 > -jnp.inf) - k)
    return n1, a1, code2, b2, tied


def _peer_kernel(h2_ref, x1_ref, gt2_ref, wqt_ref, keys_ref, cidx_ref, g_post_ref, u_ref, vt_ref,
                 y_ref,
                 qt_s, n1_s, a1_s, r2_s, b2_s, *piece_s):
    j = pl.program_id(1)
    tb = h2_ref.shape[0]
    ec = u_ref.shape[0]
    rows_per_step = ec // N_KEYS
    n_sub = tb // LANES
    gdt = r2_s.dtype
    pack = SUBLANES * (4 // jnp.dtype(gdt).itemsize)
    mxu_cols, mxu_rows = _piece_shape(tb, ec)
    pieces = [(s0, e0) for s0 in range(0, tb, mxu_cols) for e0 in range(0, ec, mxu_rows)]
    a_p = piece_s[:len(pieces)]
    w_p = piece_s[len(pieces):2 * len(pieces)]
    acc_p = piece_s[2 * len(pieces):]

    @pl.when(j == 0)
    def _route():
        qt = lax.dot_general(wqt_ref[...], h2_ref[...], _NT, preferred_element_type=jnp.float32)
        qt_s[:, :tb] = _bf16(qt)
        for acc in acc_p:
            acc[...] = jnp.zeros_like(acc)

        def sub_body(t, carry):
            lanes = pl.ds(pl.multiple_of(t * LANES, LANES), LANES)
            cidx = cidx_ref[...]

            def route(exact):
                tied = jnp.zeros((1, LANES), jnp.float32)
                for h in range(PEER_HEADS):
                    s = []
                    for p in range(2):
                        hp = 2 * h + p
                        s.append(jnp.dot(keys_ref[hp], qt_s[hp * N_KEYS:(hp + 1) * N_KEYS, lanes],
                                         preferred_element_type=jnp.float32))
                    n1, a1, r2, b2, tied_h = _route_head(s[0], s[1], cidx, exact)
                    n1_s[h, :, lanes] = n1
                    a1_s[h, :, lanes] = a1
                    r2_s[h, :, lanes] = r2.astype(gdt)
                    b2_s[h, :, lanes] = b2.astype(gdt)
                    tied = tied + tied_h
                return tied

            tied = route(exact=False)

            @pl.when(jnp.max(tied) > 0.0)
            def _redo():
                route(exact=True)
            return carry

        lax.fori_loop(0, n_sub, sub_body, 0)

    c0 = pl.multiple_of(j * rows_per_step, SUBLANES)

    def mm1(k):
        s0, e0 = pieces[k]
        a_p[k][:, :mxu_cols] = lax.dot_general(
            u_ref[e0:e0 + mxu_rows, :], h2_ref[s0:s0 + mxu_cols, :], _NT,
            preferred_element_type=jnp.float32)

    def gate_piece(k):
        s0, e0 = pieces[k]
        for t in range(mxu_cols // LANES):
            lanes = slice(s0 + t * LANES, s0 + (t + 1) * LANES)
            plane = slice(t * LANES, (t + 1) * LANES)
            for r in range(mxu_rows // N_KEYS):
                c = e0 // N_KEYS + r
                slab = pl.ds(c0 + (c // SUBLANES) * SUBLANES, SUBLANES)
                rr = c % SUBLANES
                gate = [jnp.zeros((pack, LANES), gdt) for _ in range(N_KEYS // pack)]
                for h in range(PEER_HEADS):
                    n1 = jnp.broadcast_to(n1_s[h, slab, lanes][rr:rr + 1], (pack, LANES)).astype(gdt)
                    a1 = jnp.broadcast_to(a1_s[h, slab, lanes][rr:rr + 1], (pack, LANES)).astype(gdt)
                    for kk in range(N_KEYS // pack):
                        i2 = slice(kk * pack, (kk + 1) * pack)
                        hit = r2_s[h, i2, lanes] >= n1
                        gate[kk] = gate[kk] + jnp.where(hit, b2_s[h, i2, lanes], 0.0).astype(gdt) * a1
                for kk in range(N_KEYS // pack):
                    rows = slice(r * N_KEYS + kk * pack, r * N_KEYS + (kk + 1) * pack)
                    a = a_p[k][rows, plane]
                    act = a * (1.0 + lax.erf(a * np.float32(np.sqrt(0.5))))
                    w_p[k][rows, plane] = (act.astype(gdt) * gate[kk]).astype(w_p[k].dtype)

    def mm2(k):
        s0, e0 = pieces[k]
        acc_p[s0 // mxu_cols][...] += jnp.dot(vt_ref[:, e0:e0 + mxu_rows], w_p[k][:, :mxu_cols],
                                              preferred_element_type=jnp.float32)

    for k in range(len(pieces) + 2):
        if k < len(pieces):
            mm1(k)
        if 1 <= k <= len(pieces):
            gate_piece(k - 1)
        if k >= 2:
            mm2(k - 2)

    @pl.when(j == pl.num_programs(1) - 1)
    def _finish():
        gt2 = gt2_ref[0]
        for ci, acc in enumerate(acc_p):
            tok = slice(ci * mxu_cols, (ci + 1) * mxu_cols)
            g = gt2 if gt2.shape[0] == 1 else gt2[tok]
            y_ref[tok, :] = x1_ref[tok, :] + g * _rms(acc[...].T, g_post_ref[...])


def _piece_shape(tb, ec):
    return min(tb, 2 * LANES), ec // 2


def _odd_lane_tiles(n):
    tiles = -(-n // LANES)
    return (tiles + 1 - tiles % 2) * LANES


def _peer(h2, x1, mod, wqt, keys, cidx, g_post, u_bf, vt_bf, *, tb, blocks_per_mod, ec):
    t, d = h2.shape
    ne = u_bf.shape[0]
    mod_rows = mod.shape[1]
    n_gt2 = 5
    tbp = _odd_lane_tiles(tb)
    mxu_cols, mxu_rows = _piece_shape(tb, ec)
    n_col = tb // mxu_cols
    n_pieces = n_col * (ec // mxu_rows)
    colp = _odd_lane_tiles(mxu_cols)
    return pl.pallas_call(
        _peer_kernel,
        grid=(t // tb, ne // ec),
        in_specs=[
            pl.BlockSpec((tb, d), lambda i, j: (i, 0)),
            pl.BlockSpec((tb, d), lambda i, j: (i, 0)),
            pl.BlockSpec((1, mod_rows, d), lambda i, j: (i // blocks_per_mod, 0, n_gt2)),
            _const_spec(wqt.shape), _const_spec(keys.shape), _const_spec(cidx.shape),
            _const_spec(g_post.shape),
            pl.BlockSpec((ec, d), lambda i, j: (j, 0)),
            pl.BlockSpec((d, ec), lambda i, j: (0, j)),
        ],
        out_specs=pl.BlockSpec((tb, d), lambda i, j: (i, 0)),
        out_shape=jax.ShapeDtypeStruct((t, d), jnp.float32),
        scratch_shapes=[
            pltpu.VMEM((wqt.shape[0], tbp), MXU_DTYPE),
            pltpu.VMEM((PEER_HEADS, N_KEYS, tbp), jnp.float32),
            pltpu.VMEM((PEER_HEADS, N_KEYS, tbp), jnp.float32),
            pltpu.VMEM((PEER_HEADS, N_KEYS, tbp), GATE_DTYPE),
            pltpu.VMEM((PEER_HEADS, N_KEYS, tbp), GATE_DTYPE),
        ] + [pltpu.VMEM((mxu_rows, colp), jnp.float32) for _ in range(n_pieces)]
          + [pltpu.VMEM((mxu_rows, colp), MXU_DTYPE) for _ in range(n_pieces)]
          + [pltpu.VMEM((d, mxu_cols), jnp.float32) for _ in range(n_col)],
        compiler_params=pltpu.CompilerParams(
            dimension_semantics=("arbitrary", "arbitrary"), vmem_limit_bytes=VMEM_LIMIT_BYTES),
        name="peer",
    )(h2, x1, mod, wqt, keys, cidx, g_post, u_bf, vt_bf)


def kernel(x_prompt, x_sample, state_conv, c_prompt, c_sample, w_ada, b_ada, g_pre_mix, w_in, conv_w,
           sgu_norm_g, w_s, b_s, w_out, g_post_mix, g_pre_ffn, w_q, sub_keys, expert_u, expert_v,
           g_post_ffn):
    depth = w_ada.shape[0]
    assert depth == 1, "single-layer step"
    b, l, d = x_prompt.shape
    nb = x_sample.shape[0]
    assert x_sample.shape[1] == 1 and l % CHUNK == 0
    cdim = conv_w.shape[1]
    sdim = sgu_norm_g.shape[1]
    gdim = sdim // SGU_GROUPS
    assert gdim == LANES and sub_keys.shape[3] == N_KEYS and sub_keys.shape[1] == PEER_HEADS
    ne = expert_u.shape[1]
    assert ne == N_KEYS * N_KEYS

    row = lambda a: a.reshape(1, -1)
    w_ada_b = _bf16(w_ada[0])
    w_in_b = _bf16(w_in[0])
    w_out_b = _bf16(w_out[0])
    wqt_b = _bf16(w_q[0].T)
    keys_b = _bf16(sub_keys[0].reshape(2 * PEER_HEADS, N_KEYS, -1))
    cw = conv_w[0].T
    bias = jnp.repeat(b_s[0].T, gdim, axis=1)
    w00 = jnp.repeat(w_s[0][:, 0, 0], gdim).reshape(1, sdim)
    b0 = bias[0:1]
    cidx = jnp.asarray(_candidate_cells())

    c_all = jnp.concatenate([c_prompt, c_sample], axis=0)
    mod = _ada(c_all, w_ada_b, row(b_ada[0]))
    mod_p = mod[:b].reshape(b, 1, -1)
    mod_s = mod[b:]

    tm = min(512, l)
    x1p, h2p, conv_p, u_b, vt_b = _mixer_prompt(
        x_prompt, mod_p, row(g_pre_mix[0]), w_in_b, cw, row(sgu_norm_g[0]), w_s[0], bias, w_out_b,
        row(g_post_mix[0]), row(g_pre_ffn[0]), expert_u[0], expert_v[0], tm=tm)
    x1s, h2s, conv_s, vn_s = _mixer_sample(
        x_sample.reshape(nb, d), mod_s, state_conv[0].reshape(nb, -1), row(g_pre_mix[0]), w_in_b, cw,
        row(sgu_norm_g[0]), w00, b0, w_out_b, row(g_post_mix[0]), row(g_pre_ffn[0]))

    tb = min(512, l)
    ec = 1024
    yp = _peer(h2p.reshape(b * l, d), x1p.reshape(b * l, d), mod_p, wqt_b, keys_b, cidx,
               row(g_post_ffn[0]), u_b, vt_b, tb=tb, blocks_per_mod=l // tb, ec=ec)
    ys = _peer(h2s, x1s, mod_s.reshape(1, nb, -1), wqt_b, keys_b, cidx,
               row(g_post_ffn[0]), u_b, vt_b, tb=nb, blocks_per_mod=1, ec=ec)

    return (yp.reshape(b, l, d), ys.reshape(nb, 1, d),
            conv_p.reshape(1, b, CONV_W - 1, cdim), conv_s.reshape(1, nb, CONV_W - 1, cdim),
            vn_s.reshape(1, nb, 1, sdim))
```

```python
import functools

import numpy as np
import jax
import jax.numpy as jnp
from jax import lax
from jax.experimental import pallas as pl
from jax.experimental.pallas import tpu as pltpu

EPS = 1e-6
CONV_W = 3
SGU_GROUPS = 4
CHUNK = 128
PEER_HEADS = 8
N_KEYS = 128
PEER_TOPK = 16
LANES = 128
SUBLANES = 8
RANK_MARK = 2.0 ** 20
VMEM_LIMIT_BYTES = 56 * 1024 * 1024

_NT = (((1,), (1,)), ((), ()))


def _rms(x, g):
    return x * lax.rsqrt(jnp.mean(x * x, axis=-1, keepdims=True) + EPS) * g


MXU_DTYPE = jnp.bfloat16
GATE_DTYPE = jnp.float32


def _bf16(x):
    return x.astype(MXU_DTYPE)


def _ada_kernel(c_ref, w_ref, b_ref, o_ref):
    c = c_ref[...]
    s = c * (1.0 / (1.0 + jnp.exp(-c)))
    o_ref[...] = jnp.dot(_bf16(s), w_ref[...], preferred_element_type=jnp.float32) + b_ref[...]


def _ada(c, w_ada, b_ada):
    n, d = c.shape
    e = w_ada.shape[1]
    tn = d
    return pl.pallas_call(
        _ada_kernel,
        grid=(e // tn,),
        in_specs=[
            pl.BlockSpec((n, d), lambda j: (0, 0)),
            pl.BlockSpec((d, tn), lambda j: (0, j)),
            pl.BlockSpec((1, tn), lambda j: (0, j)),
        ],
        out_specs=pl.BlockSpec((n, tn), lambda j: (0, j)),
        out_shape=jax.ShapeDtypeStruct((n, e), jnp.float32),
        name="ada",
    )(c, w_ada, b_ada)


def _group_norm(v, gs, gdim):
    outs = []
    for g in range(SGU_GROUPS):
        vg = v[:, g * gdim:(g + 1) * gdim]
        mu = jnp.mean(vg, axis=-1, keepdims=True)
        d = vg - mu
        yn = d * lax.rsqrt(jnp.mean(d * d, axis=-1, keepdims=True) + EPS)
        outs.append(yn * gs[:, g * gdim:(g + 1) * gdim])
    return jnp.concatenate(outs, axis=-1)


def _mixer_tail(x, yc, ys, mod, d, w_out_ref, g_post_ref, g_ffn_ref, x1_ref, h2_ref):
    gt1 = mod[:, 2 * d:3 * d]
    sh2 = mod[:, 3 * d:4 * d]
    sc2 = mod[:, 4 * d:5 * d]
    mix_in = _bf16(jnp.concatenate([yc, ys], axis=-1))
    mix = jnp.dot(mix_in, w_out_ref[...], preferred_element_type=jnp.float32)
    x1 = x + gt1 * _rms(mix, g_post_ref[...])
    h2 = _rms(x1, g_ffn_ref[...]) * (1.0 + sc2) + sh2
    return x1, h2


def _mixer_prompt_kernel(x_ref, mod_ref, g_pre_ref, w_in_ref, cw_ref, gs_ref, ws_ref, bias_ref,
                         w_out_ref, g_post_ref, g_ffn_ref, eu_ref, ev_ref,
                         x1_ref, h2_ref, conv_ref, ub_ref, vtb_ref, carry_ref, *, d, cdim, sdim):
    tm = x_ref.shape[1]
    gdim = sdim // SGU_GROUPS

    @pl.when(pl.program_id(1) == 0)
    def _():
        carry_ref[...] = jnp.zeros_like(carry_ref)

    x = x_ref[0]
    mod = mod_ref[0]
    sh1 = mod[:, 0:d]
    sc1 = mod[:, d:2 * d]
    h = _rms(x, g_pre_ref[...]) * (1.0 + sc1) + sh1
    p = jnp.dot(_bf16(h), w_in_ref[...], preferred_element_type=jnp.float32)
    b_gate = p[:, 0:cdim]
    z = p[:, cdim:2 * cdim] * p[:, 2 * cdim:3 * cdim]
    u = p[:, 3 * cdim:3 * cdim + sdim]
    v = p[:, 3 * cdim + sdim:]

    zext = jnp.concatenate([carry_ref[...], z], axis=0)
    z1 = pltpu.roll(zext, 1, 0)[SUBLANES:]
    z2 = pltpu.roll(zext, 2, 0)[SUBLANES:]
    cw = cw_ref[...]
    y = z2 * cw[0:1] + z1 * cw[1:2] + z * cw[2:3]
    yc = b_gate * y
    carry_ref[...] = z[tm - SUBLANES:]
    conv_ref[0] = z[tm - (CONV_W - 1):]

    vn = _group_norm(v, gs_ref[...], gdim)
    vnb = _bf16(vn)
    row = lax.broadcasted_iota(jnp.int32, (CHUNK, CHUNK), 0)
    col = lax.broadcasted_iota(jnp.int32, (CHUNK, CHUNK), 1)
    bias = bias_ref[...]
    chunks = []
    for ch in range(tm // CHUNK):
        groups = []
        for g in range(SGU_GROUPS):
            wt = _bf16(jnp.where(row >= col, ws_ref[g], 0.0))
            groups.append(jnp.dot(wt, vnb[ch * CHUNK:(ch + 1) * CHUNK, g * gdim:(g + 1) * gdim],
                                  preferred_element_type=jnp.float32))
        chunks.append(jnp.concatenate(groups, axis=-1) + bias)
    mixed = jnp.concatenate(chunks, axis=0)
    ys = u * mixed

    x1, h2 = _mixer_tail(x, yc, ys, mod, d, w_out_ref, g_post_ref, g_ffn_ref, x1_ref, h2_ref)
    x1_ref[0] = x1
    h2_ref[0] = _bf16(h2)

    ub_ref[...] = _bf16(eu_ref[...])
    vtb_ref[...] = _bf16(ev_ref[...].T)


def _mixer_sample_kernel(x_ref, mod_ref, st_ref, g_pre_ref, w_in_ref, cw_ref, gs_ref, w00_ref, b0_ref,
                         w_out_ref, g_post_ref, g_ffn_ref,
                         x1_ref, h2_ref, conv_ref, vn_ref, *, d, cdim, sdim):
    gdim = sdim // SGU_GROUPS
    x = x_ref[...]
    mod = mod_ref[...]
    sh1 = mod[:, 0:d]
    sc1 = mod[:, d:2 * d]
    h = _rms(x, g_pre_ref[...]) * (1.0 + sc1) + sh1
    p = jnp.dot(_bf16(h), w_in_ref[...], preferred_element_type=jnp.float32)
    b_gate = p[:, 0:cdim]
    z = p[:, cdim:2 * cdim] * p[:, 2 * cdim:3 * cdim]
    u = p[:, 3 * cdim:3 * cdim + sdim]
    v = p[:, 3 * cdim + sdim:]

    st = st_ref[...]
    z2 = st[:, 0:cdim]
    z1 = st[:, cdim:]
    cw = cw_ref[...]
    y = z2 * cw[0:1] + z1 * cw[1:2] + z * cw[2:3]
    yc = b_gate * y
    conv_ref[...] = jnp.concatenate([z1, z], axis=-1)

    vn = _group_norm(v, gs_ref[...], gdim)
    vn_ref[...] = vn
    mixed = vn * w00_ref[...] + b0_ref[...]
    ys = u * mixed

    x1, h2 = _mixer_tail(x, yc, ys, mod, d, w_out_ref, g_post_ref, g_ffn_ref, x1_ref, h2_ref)
    x1_ref[...] = x1
    h2_ref[...] = _bf16(h2)


def _const_spec(shape):
    zeros = (0,) * len(shape)
    return pl.BlockSpec(shape, lambda *_: zeros)


def _mixer_prompt(x, mod, g_pre, w_in, cw, gs, ws, bias, w_out, g_post, g_ffn, eu, ev, *, tm):
    b, l, d = x.shape
    cdim = cw.shape[1]
    sdim = gs.shape[1]
    ne = eu.shape[0]
    steps_per_seq = l // tm
    rows = ne // (b * steps_per_seq)
    assert rows * b * steps_per_seq == ne and rows % LANES == 0
    kern = functools.partial(_mixer_prompt_kernel, d=d, cdim=cdim, sdim=sdim)
    tok = lambda i, s: (i, s, 0)
    return pl.pallas_call(
        kern,
        grid=(b, l // tm),
        in_specs=[
            pl.BlockSpec((1, tm, d), tok),
            pl.BlockSpec((1, 1, mod.shape[2]), lambda i, s: (i, 0, 0)),
            _const_spec(g_pre.shape), _const_spec(w_in.shape), _const_spec(cw.shape),
            _const_spec(gs.shape), _const_spec(ws.shape), _const_spec(bias.shape),
            _const_spec(w_out.shape), _const_spec(g_post.shape), _const_spec(g_ffn.shape),
            pl.BlockSpec((rows, d), lambda i, s: (i * steps_per_seq + s, 0)),
            pl.BlockSpec((rows, d), lambda i, s: (i * steps_per_seq + s, 0)),
        ],
        out_specs=[
            pl.BlockSpec((1, tm, d), tok),
            pl.BlockSpec((1, tm, d), tok),
            pl.BlockSpec((1, CONV_W - 1, cdim), lambda i, s: (i, 0, 0)),
            pl.BlockSpec((rows, d), lambda i, s: (i * steps_per_seq + s, 0)),
            pl.BlockSpec((d, rows), lambda i, s: (0, i * steps_per_seq + s)),
        ],
        out_shape=[
            jax.ShapeDtypeStruct((b, l, d), jnp.float32),
            jax.ShapeDtypeStruct((b, l, d), MXU_DTYPE),
            jax.ShapeDtypeStruct((b, CONV_W - 1, cdim), jnp.float32),
            jax.ShapeDtypeStruct((ne, d), MXU_DTYPE),
            jax.ShapeDtypeStruct((d, ne), MXU_DTYPE),
        ],
        scratch_shapes=[pltpu.VMEM((SUBLANES, cdim), jnp.float32)],
        compiler_params=pltpu.CompilerParams(
            dimension_semantics=("arbitrary", "arbitrary"), vmem_limit_bytes=VMEM_LIMIT_BYTES),
        name="mixer_prompt",
    )(x, mod, g_pre, w_in, cw, gs, ws, bias, w_out, g_post, g_ffn, eu, ev)


def _mixer_sample(x, mod, st, g_pre, w_in, cw, gs, w00, b0, w_out, g_post, g_ffn):
    n, d = x.shape
    cdim = cw.shape[1]
    sdim = gs.shape[1]
    kern = functools.partial(_mixer_sample_kernel, d=d, cdim=cdim, sdim=sdim)
    args = (x, mod, st, g_pre, w_in, cw, gs, w00, b0, w_out, g_post, g_ffn)
    return pl.pallas_call(
        kern,
        grid=(1,),
        in_specs=[_const_spec(a.shape) for a in args],
        out_specs=[_const_spec((n, d)), _const_spec((n, d)),
                   _const_spec((n, (CONV_W - 1) * cdim)), _const_spec((n, sdim))],
        out_shape=[
            jax.ShapeDtypeStruct((n, d), jnp.float32),
            jax.ShapeDtypeStruct((n, d), MXU_DTYPE),
            jax.ShapeDtypeStruct((n, (CONV_W - 1) * cdim), jnp.float32),
            jax.ShapeDtypeStruct((n, sdim), jnp.float32),
        ],
        compiler_params=pltpu.CompilerParams(vmem_limit_bytes=VMEM_LIMIT_BYTES),
        name="mixer_sample",
    )(*args)


def _candidate_cells():
    k = PEER_TOPK
    idx = np.full((10 * SUBLANES,), 1e9, np.float32)
    idx[0:k] = np.arange(k)
    idx[k:k + 8] = k + np.arange(8)
    for a in range(2, 8):
        nb = k // (a + 1)
        base = 24 + 8 * (a - 2)
        idx[base:base + nb] = a * k + np.arange(nb)
    idx[72:80] = (8 + np.arange(8)) * k
    return np.broadcast_to(idx[:, None], (idx.shape[0], LANES)).copy()


def _rank_code(a):
    return -(a + 1) * RANK_MARK


def _topk_ranks(s, exact):
    iota16 = lax.broadcasted_iota(jnp.int32, (PEER_TOPK, s.shape[1]), 0)
    sv = jnp.zeros((PEER_TOPK, s.shape[1]), jnp.float32)
    work = s
    if exact:
        iota = lax.broadcasted_iota(jnp.int32, s.shape, 0).astype(jnp.float32)
        code = jnp.full(s.shape, -jnp.inf, jnp.float32)
        for a in range(PEER_TOPK):
            mx = jnp.max(work, axis=0, keepdims=True)
            idx = jnp.min(jnp.where(work == mx, iota, float(N_KEYS)), axis=0, keepdims=True)
            hit = iota == idx
            code = jnp.where(hit, _rank_code(a), code)
            work = jnp.where(hit, -jnp.inf, work)
            sv = jnp.where(iota16 == a, mx, sv)
        return code, sv
    for a in range(PEER_TOPK):
        mx = jnp.max(work, axis=0, keepdims=True)
        work = jnp.where(work == mx, _rank_code(a), work)
        sv = jnp.where(iota16 == a, mx, sv)
    return jnp.where(work <= -RANK_MARK, work, -jnp.inf), sv


def _count(mask):
    return jnp.sum(jnp.where(mask, 1.0, 0.0), axis=0, keepdims=True)


def _route_head(s1, s2, cidx, exact_keys, exact_grid):
    k = float(PEER_TOPK)
    code1, sv1 = _topk_ranks(s1, exact_keys)
    code2, sv2 = _topk_ranks(s2, exact_keys)
    pieces = [sv2 + sv1[0:1], sv2[0:8] + sv1[1:2]]
    for a in range(2, 8):
        pieces.append(sv2[0:8] + sv1[a:a + 1])
    pieces.append(sv1[8:16] + sv2[0:1])
    cand = jnp.concatenate(pieces, axis=0)
    cand = jnp.where(cidx < 1e8, cand, -jnp.inf)
    first = cidx == 0.0
    sel = jnp.where(first, 1.0, 0.0)
    work = jnp.where(first, -jnp.inf, cand)
    for _ in range(PEER_TOPK - 1):
        mx = jnp.max(work, axis=0, keepdims=True)
        if exact_grid:
            idx = jnp.min(jnp.where(work == mx, cidx, 1e9), axis=0, keepdims=True)
            hit = cidx == idx
        else:
            hit = work == mx
        sel = jnp.where(hit, 1.0, sel)
        work = jnp.where(hit, -jnp.inf, work)
    m = cand[0:1]
    z = jnp.sum(sel * jnp.exp(cand - m), axis=0, keepdims=True)
    thr = sel * -RANK_MARK
    n1 = jnp.zeros(s1.shape, jnp.float32)
    n1 = jnp.where(code1 == _rank_code(0), jnp.sum(thr[0:16], axis=0, keepdims=True), n1)
    for a in range(1, 8):
        base = 16 + 8 * (a - 1)
        n1 = jnp.where(code1 == _rank_code(a), jnp.sum(thr[base:base + 8], axis=0, keepdims=True), n1)
    for a in range(8, 16):
        n1 = jnp.where(code1 == _rank_code(a), thr[72 + a - 8:73 + a - 8], n1)
    a1 = 0.5 * jnp.exp(s1 - sv1[0:1]) / z
    b2 = jnp.exp(s2 - sv2[0:1])
    tied_keys = jnp.abs(_count(code1 > -jnp.inf) - k) + jnp.abs(_count(code2 > -jnp.inf) - k)
    tied_grid = jnp.abs(jnp.sum(sel, axis=0, keepdims=True) - k)
    return n1, a1, code2, b2, tied_keys, tied_grid


def _peer_kernel(h2_ref, x1_ref, gt2_ref, wqt_ref, keys_ref, cidx_ref, g_post_ref, u_ref, vt_ref,
                 y_ref,
                 qt_s, n1_s, a1_s, r2_s, b2_s, *piece_s):
    j = pl.program_id(1)
    tb = h2_ref.shape[0]
    ec = u_ref.shape[0]
    rows_per_step = ec // N_KEYS
    n_sub = tb // LANES
    gdt = r2_s.dtype
    pack = SUBLANES * (4 // jnp.dtype(gdt).itemsize)
    mxu_cols, mxu_rows = _piece_shape(tb, ec)
    pieces = [(s0, e0) for s0 in range(0, tb, mxu_cols) for e0 in range(0, ec, mxu_rows)]
    a_p = piece_s[:len(pieces)]
    w_p = piece_s[len(pieces):2 * len(pieces)]
    acc_p = piece_s[2 * len(pieces):]

    @pl.when(j == 0)
    def _route():
        qt = lax.dot_general(wqt_ref[...], h2_ref[...], _NT, preferred_element_type=jnp.float32)
        qt_s[:, :tb] = _bf16(qt)
        for acc in acc_p:
            acc[...] = jnp.zeros_like(acc)

        def sub_body(t, carry):
            lanes = pl.ds(pl.multiple_of(t * LANES, LANES), LANES)
            cidx = cidx_ref[...]

            def route(exact_keys, exact_grid):
                tied_keys = jnp.zeros((1, LANES), jnp.float32)
                tied_grid = jnp.zeros((1, LANES), jnp.float32)
                for h in range(PEER_HEADS):
                    s = []
                    for p in range(2):
                        hp = 2 * h + p
                        s.append(jnp.dot(keys_ref[hp], qt_s[hp * N_KEYS:(hp + 1) * N_KEYS, lanes],
                                         preferred_element_type=jnp.float32))
                    n1, a1, r2, b2, tk, tg = _route_head(s[0], s[1], cidx, exact_keys, exact_grid)
                    n1_s[h, :, lanes] = n1
                    a1_s[h, :, lanes] = a1
                    r2_s[h, :, lanes] = r2.astype(gdt)
                    b2_s[h, :, lanes] = b2.astype(gdt)
                    tied_keys = tied_keys + tk
                    tied_grid = tied_grid + tg
                return jnp.max(tied_keys) > 0.0, jnp.max(tied_grid) > 0.0

            keys_tied, grid_tied = route(False, False)

            @pl.when(keys_tied)
            def _redo_all():
                route(True, True)

            @pl.when(jnp.logical_and(jnp.logical_not(keys_tied), grid_tied))
            def _redo_grid():
                route(False, True)
            return carry

        lax.fori_loop(0, n_sub, sub_body, 0)

    c0 = pl.multiple_of(j * rows_per_step, SUBLANES)

    def mm1(k):
        s0, e0 = pieces[k]
        a_p[k][:, :mxu_cols] = lax.dot_general(
            u_ref[e0:e0 + mxu_rows, :], h2_ref[s0:s0 + mxu_cols, :], _NT,
            preferred_element_type=jnp.float32)

    def gate_piece(k):
        s0, e0 = pieces[k]
        for t in range(mxu_cols // LANES):
            lanes = slice(s0 + t * LANES, s0 + (t + 1) * LANES)
            plane = slice(t * LANES, (t + 1) * LANES)
            for r in range(mxu_rows // N_KEYS):
                c = e0 // N_KEYS + r
                slab = pl.ds(c0 + (c // SUBLANES) * SUBLANES, SUBLANES)
                rr = c % SUBLANES
                gate = [jnp.zeros((pack, LANES), gdt) for _ in range(N_KEYS // pack)]
                for h in range(PEER_HEADS):
                    n1 = jnp.broadcast_to(n1_s[h, slab, lanes][rr:rr + 1], (pack, LANES)).astype(gdt)
                    a1 = jnp.broadcast_to(a1_s[h, slab, lanes][rr:rr + 1], (pack, LANES)).astype(gdt)
                    for kk in range(N_KEYS // pack):
                        i2 = slice(kk * pack, (kk + 1) * pack)
                        hit = r2_s[h, i2, lanes] >= n1
                        gate[kk] = gate[kk] + jnp.where(hit, b2_s[h, i2, lanes], 0.0).astype(gdt) * a1
                for kk in range(N_KEYS // pack):
                    rows = slice(r * N_KEYS + kk * pack, r * N_KEYS + (kk + 1) * pack)
                    a = a_p[k][rows, plane]
                    act = a * (1.0 + lax.erf(a * np.float32(np.sqrt(0.5))))
                    w_p[k][rows, plane] = (act.astype(gdt) * gate[kk]).astype(w_p[k].dtype)

    def mm2(k):
        s0, e0 = pieces[k]
        acc_p[s0 // mxu_cols][...] += jnp.dot(vt_ref[:, e0:e0 + mxu_rows], w_p[k][:, :mxu_cols],
                                              preferred_element_type=jnp.float32)

    for k in range(len(pieces) + 2):
        if k < len(pieces):
            mm1(k)
        if 1 <= k <= len(pieces):
            gate_piece(k - 1)
        if k >= 2:
            mm2(k - 2)

    @pl.when(j == pl.num_programs(1) - 1)
    def _finish():
        gt2 = gt2_ref[0]
        for ci, acc in enumerate(acc_p):
            tok = slice(ci * mxu_cols, (ci + 1) * mxu_cols)
            g = gt2 if gt2.shape[0] == 1 else gt2[tok]
            y_ref[tok, :] = x1_ref[tok, :] + g * _rms(acc[...].T, g_post_ref[...])


def _piece_shape(tb, ec):
    return min(tb, 2 * LANES), ec // 2


def _odd_lane_tiles(n):
    tiles = -(-n // LANES)
    return (tiles + 1 - tiles % 2) * LANES


def _peer(h2, x1, mod, wqt, keys, cidx, g_post, u_bf, vt_bf, *, tb, blocks_per_mod, ec):
    t, d = h2.shape
    ne = u_bf.shape[0]
    mod_rows = mod.shape[1]
    n_gt2 = 5
    tbp = _odd_lane_tiles(tb)
    mxu_cols, mxu_rows = _piece_shape(tb, ec)
    n_col = tb // mxu_cols
    n_pieces = n_col * (ec // mxu_rows)
    colp = _odd_lane_tiles(mxu_cols)
    return pl.pallas_call(
        _peer_kernel,
        grid=(t // tb, ne // ec),
        in_specs=[
            pl.BlockSpec((tb, d), lambda i, j: (i, 0)),
            pl.BlockSpec((tb, d), lambda i, j: (i, 0)),
            pl.BlockSpec((1, mod_rows, d), lambda i, j: (i // blocks_per_mod, 0, n_gt2)),
            _const_spec(wqt.shape), _const_spec(keys.shape), _const_spec(cidx.shape),
            _const_spec(g_post.shape),
            pl.BlockSpec((ec, d), lambda i, j: (j, 0)),
            pl.BlockSpec((d, ec), lambda i, j: (0, j)),
        ],
        out_specs=pl.BlockSpec((tb, d), lambda i, j: (i, 0)),
        out_shape=jax.ShapeDtypeStruct((t, d), jnp.float32),
        scratch_shapes=[
            pltpu.VMEM((wqt.shape[0], tbp), MXU_DTYPE),
            pltpu.VMEM((PEER_HEADS, N_KEYS, tbp), jnp.float32),
            pltpu.VMEM((PEER_HEADS, N_KEYS, tbp), jnp.float32),
            pltpu.VMEM((PEER_HEADS, N_KEYS, tbp), GATE_DTYPE),
            pltpu.VMEM((PEER_HEADS, N_KEYS, tbp), GATE_DTYPE),
        ] + [pltpu.VMEM((mxu_rows, colp), jnp.float32) for _ in range(n_pieces)]
          + [pltpu.VMEM((mxu_rows, colp), MXU_DTYPE) for _ in range(n_pieces)]
          + [pltpu.VMEM((d, mxu_cols), jnp.float32) for _ in range(n_col)],
        compiler_params=pltpu.CompilerParams(
            dimension_semantics=("arbitrary", "arbitrary"), vmem_limit_bytes=VMEM_LIMIT_BYTES),
        name="peer",
    )(h2, x1, mod, wqt, keys, cidx, g_post, u_bf, vt_bf)


def _tile_sizes(l):
    return min(512, l), min(512, l), 1024


def kernel(x_prompt, x_sample, state_conv, c_prompt, c_sample, w_ada, b_ada, g_pre_mix, w_in, conv_w,
           sgu_norm_g, w_s, b_s, w_out, g_post_mix, g_pre_ffn, w_q, sub_keys, expert_u, expert_v,
           g_post_ffn):
    depth = w_ada.shape[0]
    assert depth == 1, "single-layer step"
    b, l, d = x_prompt.shape
    nb = x_sample.shape[0]
    assert x_sample.shape[1] == 1 and l % CHUNK == 0
    cdim = conv_w.shape[1]
    sdim = sgu_norm_g.shape[1]
    gdim = sdim // SGU_GROUPS
    assert gdim == LANES and sub_keys.shape[3] == N_KEYS and sub_keys.shape[1] == PEER_HEADS
    ne = expert_u.shape[1]
    assert ne == N_KEYS * N_KEYS

    row = lambda a: a.reshape(1, -1)
    w_ada_b = _bf16(w_ada[0])
    w_in_b = _bf16(w_in[0])
    w_out_b = _bf16(w_out[0])
    wqt_b = _bf16(w_q[0].T)
    keys_b = _bf16(sub_keys[0].reshape(2 * PEER_HEADS, N_KEYS, -1))
    cw = conv_w[0].T
    bias = jnp.repeat(b_s[0].T, gdim, axis=1)
    w00 = jnp.repeat(w_s[0][:, 0, 0], gdim).reshape(1, sdim)
    b0 = bias[0:1]
    cidx = jnp.asarray(_candidate_cells())

    c_all = jnp.concatenate([c_prompt, c_sample], axis=0)
    mod = _ada(c_all, w_ada_b, row(b_ada[0]))
    mod_p = mod[:b].reshape(b, 1, -1)
    mod_s = mod[b:]

    tm, tb, ec = _tile_sizes(l)
    x1p, h2p, conv_p, u_b, vt_b = _mixer_prompt(
        x_prompt, mod_p, row(g_pre_mix[0]), w_in_b, cw, row(sgu_norm_g[0]), w_s[0], bias, w_out_b,
        row(g_post_mix[0]), row(g_pre_ffn[0]), expert_u[0], expert_v[0], tm=tm)
    x1s, h2s, conv_s, vn_s = _mixer_sample(
        x_sample.reshape(nb, d), mod_s, state_conv[0].reshape(nb, -1), row(g_pre_mix[0]), w_in_b, cw,
        row(sgu_norm_g[0]), w00, b0, w_out_b, row(g_post_mix[0]), row(g_pre_ffn[0]))

    yp = _peer(h2p.reshape(b * l, d), x1p.reshape(b * l, d), mod_p, wqt_b, keys_b, cidx,
               row(g_post_ffn[0]), u_b, vt_b, tb=tb, blocks_per_mod=l // tb, ec=ec)
    ys = _peer(h2s, x1s, mod_s.reshape(1, nb, -1), wqt_b, keys_b, cidx,
               row(g_post_ffn[0]), u_b, vt_b, tb=nb, blocks_per_mod=1, ec=ec)

    return (yp.reshape(b, l, d), ys.reshape(nb, 1, d),
            conv_p.reshape(1, b, CONV_W - 1, cdim), conv_s.reshape(1, nb, CONV_W - 1, cdim),
            vn_s.reshape(1, nb, 1, sdim))
```

```python
import functools

import numpy as np
import jax
import jax.numpy as jnp
from jax import lax
from jax.experimental import pallas as pl
from jax.experimental.pallas import tpu as pltpu

EPS = 1e-6
CONV_W = 3
SGU_GROUPS = 4
CHUNK = 128
PEER_HEADS = 8
N_KEYS = 128
PEER_TOPK = 16
LANES = 128
SUBLANES = 8
RANK_MARK = 2.0 ** 20
VMEM_LIMIT_BYTES = 56 * 1024 * 1024

_NT = (((1,), (1,)), ((), ()))


def _rms(x, g):
    return x * lax.rsqrt(jnp.mean(x * x, axis=-1, keepdims=True) + EPS) * g


MXU_DTYPE = jnp.bfloat16
GATE_DTYPE = jnp.float32


def _bf16(x):
    return x.astype(MXU_DTYPE)


def _ada_kernel(c_ref, w_ref, b_ref, o_ref):
    c = c_ref[...]
    s = c * (1.0 / (1.0 + jnp.exp(-c)))
    o_ref[...] = jnp.dot(_bf16(s), w_ref[...], preferred_element_type=jnp.float32) + b_ref[...]


def _ada(c, w_ada, b_ada):
    n, d = c.shape
    e = w_ada.shape[1]
    tn = d
    return pl.pallas_call(
        _ada_kernel,
        grid=(e // tn,),
        in_specs=[
            pl.BlockSpec((n, d), lambda j: (0, 0)),
            pl.BlockSpec((d, tn), lambda j: (0, j)),
            pl.BlockSpec((1, tn), lambda j: (0, j)),
        ],
        out_specs=pl.BlockSpec((n, tn), lambda j: (0, j)),
        out_shape=jax.ShapeDtypeStruct((n, e), jnp.float32),
        name="ada",
    )(c, w_ada, b_ada)


def _group_norm(v, gs, gdim):
    outs = []
    for g in range(SGU_GROUPS):
        vg = v[:, g * gdim:(g + 1) * gdim]
        mu = jnp.mean(vg, axis=-1, keepdims=True)
        d = vg - mu
        yn = d * lax.rsqrt(jnp.mean(d * d, axis=-1, keepdims=True) + EPS)
        outs.append(yn * gs[:, g * gdim:(g + 1) * gdim])
    return jnp.concatenate(outs, axis=-1)


def _mixer_tail(x, yc, ys, mod, d, w_out_ref, g_post_ref, g_ffn_ref, x1_ref, h2_ref):
    gt1 = mod[:, 2 * d:3 * d]
    sh2 = mod[:, 3 * d:4 * d]
    sc2 = mod[:, 4 * d:5 * d]
    mix_in = _bf16(jnp.concatenate([yc, ys], axis=-1))
    mix = jnp.dot(mix_in, w_out_ref[...], preferred_element_type=jnp.float32)
    x1 = x + gt1 * _rms(mix, g_post_ref[...])
    h2 = _rms(x1, g_ffn_ref[...]) * (1.0 + sc2) + sh2
    return x1, h2


def _mixer_prompt_kernel(x_ref, mod_ref, g_pre_ref, w_in_ref, cw_ref, gs_ref, ws_ref, bias_ref,
                         w_out_ref, g_post_ref, g_ffn_ref, eu_ref, ev_ref,
                         x1_ref, h2_ref, conv_ref, ub_ref, vtb_ref, carry_ref, *, d, cdim, sdim):
    tm = x_ref.shape[1]
    gdim = sdim // SGU_GROUPS

    @pl.when(pl.program_id(1) == 0)
    def _():
        carry_ref[...] = jnp.zeros_like(carry_ref)

    x = x_ref[0]
    mod = mod_ref[0]
    sh1 = mod[:, 0:d]
    sc1 = mod[:, d:2 * d]
    h = _rms(x, g_pre_ref[...]) * (1.0 + sc1) + sh1
    p = jnp.dot(_bf16(h), w_in_ref[...], preferred_element_type=jnp.float32)
    b_gate = p[:, 0:cdim]
    z = p[:, cdim:2 * cdim] * p[:, 2 * cdim:3 * cdim]
    u = p[:, 3 * cdim:3 * cdim + sdim]
    v = p[:, 3 * cdim + sdim:]

    zext = jnp.concatenate([carry_ref[...], z], axis=0)
    z1 = pltpu.roll(zext, 1, 0)[SUBLANES:]
    z2 = pltpu.roll(zext, 2, 0)[SUBLANES:]
    cw = cw_ref[...]
    y = z2 * cw[0:1] + z1 * cw[1:2] + z * cw[2:3]
    yc = b_gate * y
    carry_ref[...] = z[tm - SUBLANES:]
    conv_ref[0] = z[tm - (CONV_W - 1):]

    vn = _group_norm(v, gs_ref[...], gdim)
    vnb = _bf16(vn)
    row = lax.broadcasted_iota(jnp.int32, (CHUNK, CHUNK), 0)
    col = lax.broadcasted_iota(jnp.int32, (CHUNK, CHUNK), 1)
    bias = bias_ref[...]
    chunks = []
    for ch in range(tm // CHUNK):
        groups = []
        for g in range(SGU_GROUPS):
            wt = _bf16(jnp.where(row >= col, ws_ref[g], 0.0))
            groups.append(jnp.dot(wt, vnb[ch * CHUNK:(ch + 1) * CHUNK, g * gdim:(g + 1) * gdim],
                                  preferred_element_type=jnp.float32))
        chunks.append(jnp.concatenate(groups, axis=-1) + bias)
    mixed = jnp.concatenate(chunks, axis=0)
    ys = u * mixed

    x1, h2 = _mixer_tail(x, yc, ys, mod, d, w_out_ref, g_post_ref, g_ffn_ref, x1_ref, h2_ref)
    x1_ref[0] = x1
    h2_ref[0] = _bf16(h2)

    ub_ref[...] = _bf16(eu_ref[...])
    vtb_ref[...] = _bf16(ev_ref[...].T)


def _mixer_sample_kernel(x_ref, mod_ref, st_ref, g_pre_ref, w_in_ref, cw_ref, gs_ref, w00_ref, b0_ref,
                         w_out_ref, g_post_ref, g_ffn_ref,
                         x1_ref, h2_ref, conv_ref, vn_ref, *, d, cdim, sdim):
    gdim = sdim // SGU_GROUPS
    x = x_ref[...]
    mod = mod_ref[...]
    sh1 = mod[:, 0:d]
    sc1 = mod[:, d:2 * d]
    h = _rms(x, g_pre_ref[...]) * (1.0 + sc1) + sh1
    p = jnp.dot(_bf16(h), w_in_ref[...], preferred_element_type=jnp.float32)
    b_gate = p[:, 0:cdim]
    z = p[:, cdim:2 * cdim] * p[:, 2 * cdim:3 * cdim]
    u = p[:, 3 * cdim:3 * cdim + sdim]
    v = p[:, 3 * cdim + sdim:]

    st = st_ref[...]
    z2 = st[:, 0:cdim]
    z1 = st[:, cdim:]
    cw = cw_ref[...]
    y = z2 * cw[0:1] + z1 * cw[1:2] + z * cw[2:3]
    yc = b_gate * y
    conv_ref[...] = jnp.concatenate([z1, z], axis=-1)

    vn = _group_norm(v, gs_ref[...], gdim)
    vn_ref[...] = vn
    mixed = vn * w00_ref[...] + b0_ref[...]
    ys = u * mixed

    x1, h2 = _mixer_tail(x, yc, ys, mod, d, w_out_ref, g_post_ref, g_ffn_ref, x1_ref, h2_ref)
    x1_ref[...] = x1
    h2_ref[...] = _bf16(h2)


def _const_spec(shape):
    zeros = (0,) * len(shape)
    return pl.BlockSpec(shape, lambda *_: zeros)


def _mixer_prompt(x, mod, g_pre, w_in, cw, gs, ws, bias, w_out, g_post, g_ffn, eu, ev, *, tm):
    b, l, d = x.shape
    cdim = cw.shape[1]
    sdim = gs.shape[1]
    ne = eu.shape[0]
    steps_per_seq = l // tm
    rows = ne // (b * steps_per_seq)
    assert rows * b * steps_per_seq == ne and rows % LANES == 0
    kern = functools.partial(_mixer_prompt_kernel, d=d, cdim=cdim, sdim=sdim)
    tok = lambda i, s: (i, s, 0)
    return pl.pallas_call(
        kern,
        grid=(b, l // tm),
        in_specs=[
            pl.BlockSpec((1, tm, d), tok),
            pl.BlockSpec((1, 1, mod.shape[2]), lambda i, s: (i, 0, 0)),
            _const_spec(g_pre.shape), _const_spec(w_in.shape), _const_spec(cw.shape),
            _const_spec(gs.shape), _const_spec(ws.shape), _const_spec(bias.shape),
            _const_spec(w_out.shape), _const_spec(g_post.shape), _const_spec(g_ffn.shape),
            pl.BlockSpec((rows, d), lambda i, s: (i * steps_per_seq + s, 0)),
            pl.BlockSpec((rows, d), lambda i, s: (i * steps_per_seq + s, 0)),
        ],
        out_specs=[
            pl.BlockSpec((1, tm, d), tok),
            pl.BlockSpec((1, tm, d), tok),
            pl.BlockSpec((1, CONV_W - 1, cdim), lambda i, s: (i, 0, 0)),
            pl.BlockSpec((rows, d), lambda i, s: (i * steps_per_seq + s, 0)),
            pl.BlockSpec((d, rows), lambda i, s: (0, i * steps_per_seq + s)),
        ],
        out_shape=[
            jax.ShapeDtypeStruct((b, l, d), jnp.float32),
            jax.ShapeDtypeStruct((b, l, d), MXU_DTYPE),
            jax.ShapeDtypeStruct((b, CONV_W - 1, cdim), jnp.float32),
            jax.ShapeDtypeStruct((ne, d), MXU_DTYPE),
            jax.ShapeDtypeStruct((d, ne), MXU_DTYPE),
        ],
        scratch_shapes=[pltpu.VMEM((SUBLANES, cdim), jnp.float32)],
        compiler_params=pltpu.CompilerParams(
            dimension_semantics=("arbitrary", "arbitrary"), vmem_limit_bytes=VMEM_LIMIT_BYTES),
        name="mixer_prompt",
    )(x, mod, g_pre, w_in, cw, gs, ws, bias, w_out, g_post, g_ffn, eu, ev)


def _mixer_sample(x, mod, st, g_pre, w_in, cw, gs, w00, b0, w_out, g_post, g_ffn):
    n, d = x.shape
    cdim = cw.shape[1]
    sdim = gs.shape[1]
    kern = functools.partial(_mixer_sample_kernel, d=d, cdim=cdim, sdim=sdim)
    args = (x, mod, st, g_pre, w_in, cw, gs, w00, b0, w_out, g_post, g_ffn)
    return pl.pallas_call(
        kern,
        grid=(1,),
        in_specs=[_const_spec(a.shape) for a in args],
        out_specs=[_const_spec((n, d)), _const_spec((n, d)),
                   _const_spec((n, (CONV_W - 1) * cdim)), _const_spec((n, sdim))],
        out_shape=[
            jax.ShapeDtypeStruct((n, d), jnp.float32),
            jax.ShapeDtypeStruct((n, d), MXU_DTYPE),
            jax.ShapeDtypeStruct((n, (CONV_W - 1) * cdim), jnp.float32),
            jax.ShapeDtypeStruct((n, sdim), jnp.float32),
        ],
        compiler_params=pltpu.CompilerParams(vmem_limit_bytes=VMEM_LIMIT_BYTES),
        name="mixer_sample",
    )(*args)


def _candidate_cells():
    k = PEER_TOPK
    idx = np.full((10 * SUBLANES,), 1e9, np.float32)
    idx[0:k] = np.arange(k)
    idx[k:k + 8] = k + np.arange(8)
    for a in range(2, 8):
        nb = k // (a + 1)
        base = 24 + 8 * (a - 2)
        idx[base:base + nb] = a * k + np.arange(nb)
    idx[72:80] = (8 + np.arange(8)) * k
    return np.broadcast_to(idx[:, None], (idx.shape[0], LANES)).copy()


def _rank_code(a):
    return -(a + 1) * RANK_MARK


def _topk_ranks(s, exact):
    iota16 = lax.broadcasted_iota(jnp.int32, (PEER_TOPK, s.shape[1]), 0)
    sv = jnp.zeros((PEER_TOPK, s.shape[1]), jnp.float32)
    work = s
    if exact:
        iota = lax.broadcasted_iota(jnp.int32, s.shape, 0).astype(jnp.float32)
        code = jnp.full(s.shape, -jnp.inf, jnp.float32)
        for a in range(PEER_TOPK):
            mx = jnp.max(work, axis=0, keepdims=True)
            idx = jnp.min(jnp.where(work == mx, iota, float(N_KEYS)), axis=0, keepdims=True)
            hit = iota == idx
            code = jnp.where(hit, _rank_code(a), code)
            work = jnp.where(hit, -jnp.inf, work)
            sv = jnp.where(iota16 == a, mx, sv)
        return code, sv
    for a in range(PEER_TOPK):
        mx = jnp.max(work, axis=0, keepdims=True)
        work = jnp.where(work == mx, _rank_code(a), work)
        sv = jnp.where(iota16 == a, mx, sv)
    return jnp.where(work <= -RANK_MARK, work, -jnp.inf), sv


def _count(mask):
    return jnp.sum(jnp.where(mask, 1.0, 0.0), axis=0, keepdims=True)


def _grid_stage(code1, e1, sv1, sv2, cidx, exact):
    pieces = [sv2 + sv1[0:1], sv2[0:8] + sv1[1:2]]
    for a in range(2, 8):
        pieces.append(sv2[0:8] + sv1[a:a + 1])
    pieces.append(sv1[8:16] + sv2[0:1])
    cand = jnp.concatenate(pieces, axis=0)
    cand = jnp.where(cidx < 1e8, cand, -jnp.inf)
    first = cidx == 0.0
    sel = jnp.where(first, 1.0, 0.0)
    work = jnp.where(first, -jnp.inf, cand)
    for _ in range(PEER_TOPK - 1):
        mx = jnp.max(work, axis=0, keepdims=True)
        if exact:
            idx = jnp.min(jnp.where(work == mx, cidx, 1e9), axis=0, keepdims=True)
            hit = cidx == idx
        else:
            hit = work == mx
        sel = jnp.where(hit, 1.0, sel)
        work = jnp.where(hit, -jnp.inf, work)
    m = cand[0:1]
    z = jnp.sum(sel * jnp.exp(cand - m), axis=0, keepdims=True)
    thr = sel * -RANK_MARK
    n1 = jnp.zeros(code1.shape, jnp.float32)
    n1 = jnp.where(code1 == _rank_code(0), jnp.sum(thr[0:16], axis=0, keepdims=True), n1)
    for a in range(1, 8):
        base = 16 + 8 * (a - 1)
        n1 = jnp.where(code1 == _rank_code(a), jnp.sum(thr[base:base + 8], axis=0, keepdims=True), n1)
    for a in range(8, 16):
        n1 = jnp.where(code1 == _rank_code(a), thr[72 + a - 8:73 + a - 8], n1)
    a1 = 0.5 * e1 / z
    tied = jnp.abs(jnp.sum(sel, axis=0, keepdims=True) - float(PEER_TOPK))
    return n1, a1, tied


def _peer_kernel(h2_ref, x1_ref, gt2_ref, wqt_ref, keys_ref, cidx_ref, g_post_ref, u_ref, vt_ref,
                 y_ref,
                 qt_s, n1_s, a1_s, r2_s, b2_s, c1_s, e1_s, sv_s, *piece_s):
    j = pl.program_id(1)
    tb = h2_ref.shape[0]
    ec = u_ref.shape[0]
    rows_per_step = ec // N_KEYS
    n_sub = tb // LANES
    gdt = r2_s.dtype
    pack = SUBLANES * (4 // jnp.dtype(gdt).itemsize)
    mxu_cols, mxu_rows = _piece_shape(tb, ec)
    pieces = [(s0, e0) for s0 in range(0, tb, mxu_cols) for e0 in range(0, ec, mxu_rows)]
    a_p = piece_s[:len(pieces)]
    w_p = piece_s[len(pieces):2 * len(pieces)]
    acc_p = piece_s[2 * len(pieces):]

    @pl.when(j == 0)
    def _route():
        qt = lax.dot_general(wqt_ref[...], h2_ref[...], _NT, preferred_element_type=jnp.float32)
        qt_s[:, :tb] = _bf16(qt)
        for acc in acc_p:
            acc[...] = jnp.zeros_like(acc)

        def sub_body(t, carry):
            lanes = pl.ds(pl.multiple_of(t * LANES, LANES), LANES)
            cidx = cidx_ref[...]

            def route(exact):
                k = float(PEER_TOPK)
                tied_keys = jnp.zeros((1, LANES), jnp.float32)
                tied_grid = jnp.zeros((1, LANES), jnp.float32)
                for h in range(PEER_HEADS):
                    s = []
                    for p in range(2):
                        hp = 2 * h + p
                        s.append(jnp.dot(keys_ref[hp], qt_s[hp * N_KEYS:(hp + 1) * N_KEYS, lanes],
                                         preferred_element_type=jnp.float32))
                    code1, sv1 = _topk_ranks(s[0], exact)
                    code2, sv2 = _topk_ranks(s[1], exact)
                    e1 = jnp.exp(s[0] - sv1[0:1])
                    n1, a1, tg = _grid_stage(code1, e1, sv1, sv2, cidx, exact)
                    n1_s[h, :, lanes] = n1
                    a1_s[h, :, lanes] = a1
                    r2_s[h, :, lanes] = code2.astype(gdt)
                    b2_s[h, :, lanes] = jnp.exp(s[1] - sv2[0:1]).astype(gdt)
                    c1_s[h, :, lanes] = code1
                    e1_s[h, :, lanes] = e1
                    sv_s[h, 0:PEER_TOPK, lanes] = sv1
                    sv_s[h, PEER_TOPK:2 * PEER_TOPK, lanes] = sv2
                    tied_keys = (tied_keys + jnp.abs(_count(code1 > -jnp.inf) - k)
                                 + jnp.abs(_count(code2 > -jnp.inf) - k))
                    tied_grid = tied_grid + tg
                return jnp.max(tied_keys) > 0.0, jnp.max(tied_grid) > 0.0

            keys_tied, grid_tied = route(False)

            @pl.when(keys_tied)
            def _redo_all():
                route(True)

            @pl.when(jnp.logical_and(jnp.logical_not(keys_tied), grid_tied))
            def _redo_grid():
                for h in range(PEER_HEADS):
                    n1, a1, _ = _grid_stage(c1_s[h, :, lanes], e1_s[h, :, lanes],
                                            sv_s[h, 0:PEER_TOPK, lanes],
                                            sv_s[h, PEER_TOPK:2 * PEER_TOPK, lanes], cidx, True)
                    n1_s[h, :, lanes] = n1
                    a1_s[h, :, lanes] = a1
            return carry

        lax.fori_loop(0, n_sub, sub_body, 0)

    c0 = pl.multiple_of(j * rows_per_step, SUBLANES)

    def mm1(k):
        s0, e0 = pieces[k]
        a_p[k][:, :mxu_cols] = lax.dot_general(
            u_ref[e0:e0 + mxu_rows, :], h2_ref[s0:s0 + mxu_cols, :], _NT,
            preferred_element_type=jnp.float32)

    def gate_piece(k):
        s0, e0 = pieces[k]
        for t in range(mxu_cols // LANES):
            lanes = slice(s0 + t * LANES, s0 + (t + 1) * LANES)
            plane = slice(t * LANES, (t + 1) * LANES)
            for r in range(mxu_rows // N_KEYS):
                c = e0 // N_KEYS + r
                slab = pl.ds(c0 + (c // SUBLANES) * SUBLANES, SUBLANES)
                rr = c % SUBLANES
                gate = [jnp.zeros((pack, LANES), gdt) for _ in range(N_KEYS // pack)]
                for h in range(PEER_HEADS):
                    n1 = jnp.broadcast_to(n1_s[h, slab, lanes][rr:rr + 1], (pack, LANES)).astype(gdt)
                    a1 = jnp.broadcast_to(a1_s[h, slab, lanes][rr:rr + 1], (pack, LANES)).astype(gdt)
                    for kk in range(N_KEYS // pack):
                        i2 = slice(kk * pack, (kk + 1) * pack)
                        hit = r2_s[h, i2, lanes] >= n1
                        gate[kk] = gate[kk] + jnp.where(hit, b2_s[h, i2, lanes], 0.0).astype(gdt) * a1
                for kk in range(N_KEYS // pack):
                    rows = slice(r * N_KEYS + kk * pack, r * N_KEYS + (kk + 1) * pack)
                    a = a_p[k][rows, plane]
                    act = a * (1.0 + lax.erf(a * np.float32(np.sqrt(0.5))))
                    w_p[k][rows, plane] = (act.astype(gdt) * gate[kk]).astype(w_p[k].dtype)

    def mm2(k):
        s0, e0 = pieces[k]
        acc_p[s0 // mxu_cols][...] += jnp.dot(vt_ref[:, e0:e0 + mxu_rows], w_p[k][:, :mxu_cols],
                                              preferred_element_type=jnp.float32)

    for k in range(len(pieces) + 2):
        if k < len(pieces):
            mm1(k)
        if 1 <= k <= len(pieces):
            gate_piece(k - 1)
        if k >= 2:
            mm2(k - 2)

    @pl.when(j == pl.num_programs(1) - 1)
    def _finish():
        gt2 = gt2_ref[0]
        for ci, acc in enumerate(acc_p):
            tok = slice(ci * mxu_cols, (ci + 1) * mxu_cols)
            g = gt2 if gt2.shape[0] == 1 else gt2[tok]
            y_ref[tok, :] = x1_ref[tok, :] + g * _rms(acc[...].T, g_post_ref[...])


def _piece_shape(tb, ec):
    return min(tb, 2 * LANES), ec // 2


def _odd_lane_tiles(n):
    tiles = -(-n // LANES)
    return (tiles + 1 - tiles % 2) * LANES


def _peer(h2, x1, mod, wqt, keys, cidx, g_post, u_bf, vt_bf, *, tb, blocks_per_mod, ec):
    t, d = h2.shape
    ne = u_bf.shape[0]
    mod_rows = mod.shape[1]
    n_gt2 = 5
    tbp = _odd_lane_tiles(tb)
    mxu_cols, mxu_rows = _piece_shape(tb, ec)
    n_col = tb // mxu_cols
    n_pieces = n_col * (ec // mxu_rows)
    colp = _odd_lane_tiles(mxu_cols)
    return pl.pallas_call(
        _peer_kernel,
        grid=(t // tb, ne // ec),
        in_specs=[
            pl.BlockSpec((tb, d), lambda i, j: (i, 0)),
            pl.BlockSpec((tb, d), lambda i, j: (i, 0)),
            pl.BlockSpec((1, mod_rows, d), lambda i, j: (i // blocks_per_mod, 0, n_gt2)),
            _const_spec(wqt.shape), _const_spec(keys.shape), _const_spec(cidx.shape),
            _const_spec(g_post.shape),
            pl.BlockSpec((ec, d), lambda i, j: (j, 0)),
            pl.BlockSpec((d, ec), lambda i, j: (0, j)),
        ],
        out_specs=pl.BlockSpec((tb, d), lambda i, j: (i, 0)),
        out_shape=jax.ShapeDtypeStruct((t, d), jnp.float32),
        scratch_shapes=[
            pltpu.VMEM((wqt.shape[0], tbp), MXU_DTYPE),
            pltpu.VMEM((PEER_HEADS, N_KEYS, tbp), jnp.float32),
            pltpu.VMEM((PEER_HEADS, N_KEYS, tbp), jnp.float32),
            pltpu.VMEM((PEER_HEADS, N_KEYS, tbp), GATE_DTYPE),
            pltpu.VMEM((PEER_HEADS, N_KEYS, tbp), GATE_DTYPE),
            pltpu.VMEM((PEER_HEADS, N_KEYS, tbp), jnp.float32),
            pltpu.VMEM((PEER_HEADS, N_KEYS, tbp), jnp.float32),
            pltpu.VMEM((PEER_HEADS, 2 * PEER_TOPK, tbp), jnp.float32),
        ] + [pltpu.VMEM((mxu_rows, colp), jnp.float32) for _ in range(n_pieces)]
          + [pltpu.VMEM((mxu_rows, colp), MXU_DTYPE) for _ in range(n_pieces)]
          + [pltpu.VMEM((d, mxu_cols), jnp.float32) for _ in range(n_col)],
        compiler_params=pltpu.CompilerParams(
            dimension_semantics=("arbitrary", "arbitrary"), vmem_limit_bytes=VMEM_LIMIT_BYTES),
        name="peer",
    )(h2, x1, mod, wqt, keys, cidx, g_post, u_bf, vt_bf)


def _tile_sizes(l):
    return min(512, l), min(512, l), 1024


def kernel(x_prompt, x_sample, state_conv, c_prompt, c_sample, w_ada, b_ada, g_pre_mix, w_in, conv_w,
           sgu_norm_g, w_s, b_s, w_out, g_post_mix, g_pre_ffn, w_q, sub_keys, expert_u, expert_v,
           g_post_ffn):
    depth = w_ada.shape[0]
    assert depth == 1, "single-layer step"
    b, l, d = x_prompt.shape
    nb = x_sample.shape[0]
    assert x_sample.shape[1] == 1 and l % CHUNK == 0
    cdim = conv_w.shape[1]
    sdim = sgu_norm_g.shape[1]
    gdim = sdim // SGU_GROUPS
    assert gdim == LANES and sub_keys.shape[3] == N_KEYS and sub_keys.shape[1] == PEER_HEADS
    ne = expert_u.shape[1]
    assert ne == N_KEYS * N_KEYS

    row = lambda a: a.reshape(1, -1)
    w_ada_b = _bf16(w_ada[0])
    w_in_b = _bf16(w_in[0])
    w_out_b = _bf16(w_out[0])
    wqt_b = _bf16(w_q[0].T)
    keys_b = _bf16(sub_keys[0].reshape(2 * PEER_HEADS, N_KEYS, -1))
    cw = conv_w[0].T
    bias = jnp.repeat(b_s[0].T, gdim, axis=1)
    w00 = jnp.repeat(w_s[0][:, 0, 0], gdim).reshape(1, sdim)
    b0 = bias[0:1]
    cidx = jnp.asarray(_candidate_cells())

    c_all = jnp.concatenate([c_prompt, c_sample], axis=0)
    mod = _ada(c_all, w_ada_b, row(b_ada[0]))
    mod_p = mod[:b].reshape(b, 1, -1)
    mod_s = mod[b:]

    tm, tb, ec = _tile_sizes(l)
    x1p, h2p, conv_p, u_b, vt_b = _mixer_prompt(
        x_prompt, mod_p, row(g_pre_mix[0]), w_in_b, cw, row(sgu_norm_g[0]), w_s[0], bias, w_out_b,
        row(g_post_mix[0]), row(g_pre_ffn[0]), expert_u[0], expert_v[0], tm=tm)
    x1s, h2s, conv_s, vn_s = _mixer_sample(
        x_sample.reshape(nb, d), mod_s, state_conv[0].reshape(nb, -1), row(g_pre_mix[0]), w_in_b, cw,
        row(sgu_norm_g[0]), w00, b0, w_out_b, row(g_post_mix[0]), row(g_pre_ffn[0]))

    yp = _peer(h2p.reshape(b * l, d), x1p.reshape(b * l, d), mod_p, wqt_b, keys_b, cidx,
               row(g_post_ffn[0]), u_b, vt_b, tb=tb, blocks_per_mod=l // tb, ec=ec)
    ys = _peer(h2s, x1s, mod_s.reshape(1, nb, -1), wqt_b, keys_b, cidx,
               row(g_post_ffn[0]), u_b, vt_b, tb=nb, blocks_per_mod=1, ec=ec)

    return (yp.reshape(b, l, d), ys.reshape(nb, 1, d),
            conv_p.reshape(1, b, CONV_W - 1, cdim), conv_s.reshape(1, nb, CONV_W - 1, cdim),
            vn_s.reshape(1, nb, 1, sdim))
```

```python
import functools

import numpy as np
import jax
import jax.numpy as jnp
from jax import lax
from jax.experimental import pallas as pl
from jax.experimental.pallas import tpu as pltpu

EPS = 1e-6
CONV_W = 3
SGU_GROUPS = 4
CHUNK = 128
PEER_HEADS = 8
N_KEYS = 128
PEER_TOPK = 16
LANES = 128
SUBLANES = 8
GELU_SCALE = float(np.sqrt(0.5))
RANK_MARK = 2.0 ** 20
VMEM_LIMIT_BYTES = 56 * 1024 * 1024

_NT = (((1,), (1,)), ((), ()))


def _rms(x, g):
    return x * lax.rsqrt(jnp.mean(x * x, axis=-1, keepdims=True) + EPS) * g


MXU_DTYPE = jnp.bfloat16
GATE_DTYPE = jnp.float32


def _bf16(x):
    return x.astype(MXU_DTYPE)


def _ada_kernel(c_ref, w_ref, b_ref, o_ref):
    c = c_ref[...]
    s = c * (1.0 / (1.0 + jnp.exp(-c)))
    o_ref[...] = jnp.dot(_bf16(s), w_ref[...], preferred_element_type=jnp.float32) + b_ref[...]


def _ada(c, w_ada, b_ada):
    n, d = c.shape
    e = w_ada.shape[1]
    tn = d
    return pl.pallas_call(
        _ada_kernel,
        grid=(e // tn,),
        in_specs=[
            pl.BlockSpec((n, d), lambda j: (0, 0)),
            pl.BlockSpec((d, tn), lambda j: (0, j)),
            pl.BlockSpec((1, tn), lambda j: (0, j)),
        ],
        out_specs=pl.BlockSpec((n, tn), lambda j: (0, j)),
        out_shape=jax.ShapeDtypeStruct((n, e), jnp.float32),
        name="ada",
    )(c, w_ada, b_ada)


def _group_norm(v, gs, gdim):
    outs = []
    for g in range(SGU_GROUPS):
        vg = v[:, g * gdim:(g + 1) * gdim]
        mu = jnp.mean(vg, axis=-1, keepdims=True)
        d = vg - mu
        yn = d * lax.rsqrt(jnp.mean(d * d, axis=-1, keepdims=True) + EPS)
        outs.append(yn * gs[:, g * gdim:(g + 1) * gdim])
    return jnp.concatenate(outs, axis=-1)


def _mixer_tail(x, yc, ys, mod, d, w_out_ref, g_post_ref, g_ffn_ref, x1_ref, h2_ref):
    gt1 = mod[:, 2 * d:3 * d]
    sh2 = mod[:, 3 * d:4 * d]
    sc2 = mod[:, 4 * d:5 * d]
    mix_in = _bf16(jnp.concatenate([yc, ys], axis=-1))
    mix = jnp.dot(mix_in, w_out_ref[...], preferred_element_type=jnp.float32)
    x1 = x + gt1 * _rms(mix, g_post_ref[...])
    h2 = _rms(x1, g_ffn_ref[...]) * (1.0 + sc2) + sh2
    return x1, h2


def _mixer_prompt_kernel(x_ref, mod_ref, g_pre_ref, w_in_ref, cw_ref, gs_ref, ws_ref, bias_ref,
                         w_out_ref, g_post_ref, g_ffn_ref, eu_ref, ev_ref,
                         x1_ref, h2_ref, conv_ref, ub_ref, vtb_ref, carry_ref, *, d, cdim, sdim):
    tm = x_ref.shape[1]
    gdim = sdim // SGU_GROUPS

    @pl.when(pl.program_id(1) == 0)
    def _():
        carry_ref[...] = jnp.zeros_like(carry_ref)

    x = x_ref[0]
    mod = mod_ref[0]
    sh1 = mod[:, 0:d]
    sc1 = mod[:, d:2 * d]
    h = _rms(x, g_pre_ref[...]) * (1.0 + sc1) + sh1
    p = jnp.dot(_bf16(h), w_in_ref[...], preferred_element_type=jnp.float32)
    b_gate = p[:, 0:cdim]
    z = p[:, cdim:2 * cdim] * p[:, 2 * cdim:3 * cdim]
    u = p[:, 3 * cdim:3 * cdim + sdim]
    v = p[:, 3 * cdim + sdim:]

    zext = jnp.concatenate([carry_ref[...], z], axis=0)
    z1 = pltpu.roll(zext, 1, 0)[SUBLANES:]
    z2 = pltpu.roll(zext, 2, 0)[SUBLANES:]
    cw = cw_ref[...]
    y = z2 * cw[0:1] + z1 * cw[1:2] + z * cw[2:3]
    yc = b_gate * y
    carry_ref[...] = z[tm - SUBLANES:]
    conv_ref[0] = z[tm - (CONV_W - 1):]

    vn = _group_norm(v, gs_ref[...], gdim)
    vnb = _bf16(vn)
    row = lax.broadcasted_iota(jnp.int32, (CHUNK, CHUNK), 0)
    col = lax.broadcasted_iota(jnp.int32, (CHUNK, CHUNK), 1)
    bias = bias_ref[...]
    chunks = []
    for ch in range(tm // CHUNK):
        groups = []
        for g in range(SGU_GROUPS):
            wt = _bf16(jnp.where(row >= col, ws_ref[g], 0.0))
            groups.append(jnp.dot(wt, vnb[ch * CHUNK:(ch + 1) * CHUNK, g * gdim:(g + 1) * gdim],
                                  preferred_element_type=jnp.float32))
        chunks.append(jnp.concatenate(groups, axis=-1) + bias)
    mixed = jnp.concatenate(chunks, axis=0)
    ys = u * mixed

    x1, h2 = _mixer_tail(x, yc, ys, mod, d, w_out_ref, g_post_ref, g_ffn_ref, x1_ref, h2_ref)
    x1_ref[0] = x1
    h2_ref[0] = _bf16(h2)

    ub_ref[...] = _bf16(eu_ref[...] * GELU_SCALE)
    vtb_ref[...] = _bf16(ev_ref[...].T)


def _mixer_sample_kernel(x_ref, mod_ref, st_ref, g_pre_ref, w_in_ref, cw_ref, gs_ref, w00_ref, b0_ref,
                         w_out_ref, g_post_ref, g_ffn_ref,
                         x1_ref, h2_ref, conv_ref, vn_ref, *, d, cdim, sdim):
    gdim = sdim // SGU_GROUPS
    x = x_ref[...]
    mod = mod_ref[...]
    sh1 = mod[:, 0:d]
    sc1 = mod[:, d:2 * d]
    h = _rms(x, g_pre_ref[...]) * (1.0 + sc1) + sh1
    p = jnp.dot(_bf16(h), w_in_ref[...], preferred_element_type=jnp.float32)
    b_gate = p[:, 0:cdim]
    z = p[:, cdim:2 * cdim] * p[:, 2 * cdim:3 * cdim]
    u = p[:, 3 * cdim:3 * cdim + sdim]
    v = p[:, 3 * cdim + sdim:]

    st = st_ref[...]
    z2 = st[:, 0:cdim]
    z1 = st[:, cdim:]
    cw = cw_ref[...]
    y = z2 * cw[0:1] + z1 * cw[1:2] + z * cw[2:3]
    yc = b_gate * y
    conv_ref[...] = jnp.concatenate([z1, z], axis=-1)

    vn = _group_norm(v, gs_ref[...], gdim)
    vn_ref[...] = vn
    mixed = vn * w00_ref[...] + b0_ref[...]
    ys = u * mixed

    x1, h2 = _mixer_tail(x, yc, ys, mod, d, w_out_ref, g_post_ref, g_ffn_ref, x1_ref, h2_ref)
    x1_ref[...] = x1
    h2_ref[...] = _bf16(h2)


def _const_spec(shape):
    zeros = (0,) * len(shape)
    return pl.BlockSpec(shape, lambda *_: zeros)


def _mixer_prompt(x, mod, g_pre, w_in, cw, gs, ws, bias, w_out, g_post, g_ffn, eu, ev, *, tm):
    b, l, d = x.shape
    cdim = cw.shape[1]
    sdim = gs.shape[1]
    ne = eu.shape[0]
    steps_per_seq = l // tm
    rows = ne // (b * steps_per_seq)
    assert rows * b * steps_per_seq == ne and rows % LANES == 0
    kern = functools.partial(_mixer_prompt_kernel, d=d, cdim=cdim, sdim=sdim)
    tok = lambda i, s: (i, s, 0)
    return pl.pallas_call(
        kern,
        grid=(b, l // tm),
        in_specs=[
            pl.BlockSpec((1, tm, d), tok),
            pl.BlockSpec((1, 1, mod.shape[2]), lambda i, s: (i, 0, 0)),
            _const_spec(g_pre.shape), _const_spec(w_in.shape), _const_spec(cw.shape),
            _const_spec(gs.shape), _const_spec(ws.shape), _const_spec(bias.shape),
            _const_spec(w_out.shape), _const_spec(g_post.shape), _const_spec(g_ffn.shape),
            pl.BlockSpec((rows, d), lambda i, s: (i * steps_per_seq + s, 0)),
            pl.BlockSpec((rows, d), lambda i, s: (i * steps_per_seq + s, 0)),
        ],
        out_specs=[
            pl.BlockSpec((1, tm, d), tok),
            pl.BlockSpec((1, tm, d), tok),
            pl.BlockSpec((1, CONV_W - 1, cdim), lambda i, s: (i, 0, 0)),
            pl.BlockSpec((rows, d), lambda i, s: (i * steps_per_seq + s, 0)),
            pl.BlockSpec((d, rows), lambda i, s: (0, i * steps_per_seq + s)),
        ],
        out_shape=[
            jax.ShapeDtypeStruct((b, l, d), jnp.float32),
            jax.ShapeDtypeStruct((b, l, d), MXU_DTYPE),
            jax.ShapeDtypeStruct((b, CONV_W - 1, cdim), jnp.float32),
            jax.ShapeDtypeStruct((ne, d), MXU_DTYPE),
            jax.ShapeDtypeStruct((d, ne), MXU_DTYPE),
        ],
        scratch_shapes=[pltpu.VMEM((SUBLANES, cdim), jnp.float32)],
        compiler_params=pltpu.CompilerParams(
            dimension_semantics=("arbitrary", "arbitrary"), vmem_limit_bytes=VMEM_LIMIT_BYTES),
        name="mixer_prompt",
    )(x, mod, g_pre, w_in, cw, gs, ws, bias, w_out, g_post, g_ffn, eu, ev)


def _mixer_sample(x, mod, st, g_pre, w_in, cw, gs, w00, b0, w_out, g_post, g_ffn):
    n, d = x.shape
    cdim = cw.shape[1]
    sdim = gs.shape[1]
    kern = functools.partial(_mixer_sample_kernel, d=d, cdim=cdim, sdim=sdim)
    args = (x, mod, st, g_pre, w_in, cw, gs, w00, b0, w_out, g_post, g_ffn)
    return pl.pallas_call(
        kern,
        grid=(1,),
        in_specs=[_const_spec(a.shape) for a in args],
        out_specs=[_const_spec((n, d)), _const_spec((n, d)),
                   _const_spec((n, (CONV_W - 1) * cdim)), _const_spec((n, sdim))],
        out_shape=[
            jax.ShapeDtypeStruct((n, d), jnp.float32),
            jax.ShapeDtypeStruct((n, d), MXU_DTYPE),
            jax.ShapeDtypeStruct((n, (CONV_W - 1) * cdim), jnp.float32),
            jax.ShapeDtypeStruct((n, sdim), jnp.float32),
        ],
        compiler_params=pltpu.CompilerParams(vmem_limit_bytes=VMEM_LIMIT_BYTES),
        name="mixer_sample",
    )(*args)


def _candidate_cells():
    k = PEER_TOPK
    idx = np.full((10 * SUBLANES,), 1e9, np.float32)
    idx[0:k] = np.arange(k)
    idx[k:k + 8] = k + np.arange(8)
    for a in range(2, 8):
        nb = k // (a + 1)
        base = 24 + 8 * (a - 2)
        idx[base:base + nb] = a * k + np.arange(nb)
    idx[72:80] = (8 + np.arange(8)) * k
    return np.broadcast_to(idx[:, None], (idx.shape[0], LANES)).copy()


def _rank_code(a):
    return -(a + 1) * RANK_MARK


def _topk_ranks(s, exact):
    iota16 = lax.broadcasted_iota(jnp.int32, (PEER_TOPK, s.shape[1]), 0)
    sv = jnp.zeros((PEER_TOPK, s.shape[1]), jnp.float32)
    work = s
    if exact:
        iota = lax.broadcasted_iota(jnp.int32, s.shape, 0).astype(jnp.float32)
        code = jnp.full(s.shape, -jnp.inf, jnp.float32)
        for a in range(PEER_TOPK):
            mx = jnp.max(work, axis=0, keepdims=True)
            idx = jnp.min(jnp.where(work == mx, iota, float(N_KEYS)), axis=0, keepdims=True)
            hit = iota == idx
            code = jnp.where(hit, _rank_code(a), code)
            work = jnp.where(hit, -jnp.inf, work)
            sv = jnp.where(iota16 == a, mx, sv)
        return code, sv
    for a in range(PEER_TOPK):
        mx = jnp.max(work, axis=0, keepdims=True)
        work = jnp.where(work == mx, _rank_code(a), work)
        sv = jnp.where(iota16 == a, mx, sv)
    return jnp.where(work <= -RANK_MARK, work, -jnp.inf), sv


def _count(mask):
    return jnp.sum(jnp.where(mask, 1.0, 0.0), axis=0, keepdims=True)


def _grid_stage(code1, e1, sv1, sv2, cidx, exact):
    pieces = [sv2 + sv1[0:1], sv2[0:8] + sv1[1:2]]
    for a in range(2, 8):
        pieces.append(sv2[0:8] + sv1[a:a + 1])
    pieces.append(sv1[8:16] + sv2[0:1])
    cand = jnp.concatenate(pieces, axis=0)
    cand = jnp.where(cidx < 1e8, cand, -jnp.inf)
    first = cidx == 0.0
    sel = jnp.where(first, 1.0, 0.0)
    work = jnp.where(first, -jnp.inf, cand)
    for _ in range(PEER_TOPK - 1):
        mx = jnp.max(work, axis=0, keepdims=True)
        if exact:
            idx = jnp.min(jnp.where(work == mx, cidx, 1e9), axis=0, keepdims=True)
            hit = cidx == idx
        else:
            hit = work == mx
        sel = jnp.where(hit, 1.0, sel)
        work = jnp.where(hit, -jnp.inf, work)
    m = cand[0:1]
    z = jnp.sum(sel * jnp.exp(cand - m), axis=0, keepdims=True)
    thr = sel * -RANK_MARK
    n1 = jnp.zeros(code1.shape, jnp.float32)
    n1 = jnp.where(code1 == _rank_code(0), jnp.sum(thr[0:16], axis=0, keepdims=True), n1)
    for a in range(1, 8):
        base = 16 + 8 * (a - 1)
        n1 = jnp.where(code1 == _rank_code(a), jnp.sum(thr[base:base + 8], axis=0, keepdims=True), n1)
    for a in range(8, 16):
        n1 = jnp.where(code1 == _rank_code(a), thr[72 + a - 8:73 + a - 8], n1)
    a1 = GELU_SCALE * e1 / z
    tied = jnp.abs(jnp.sum(sel, axis=0, keepdims=True) - float(PEER_TOPK))
    return n1, a1, tied


def _peer_kernel(h2_ref, x1_ref, gt2_ref, wqt_ref, keys_ref, cidx_ref, g_post_ref, u_ref, vt_ref,
                 y_ref,
                 qt_s, n1_s, a1_s, r2_s, b2_s, c1_s, e1_s, sv_s, *piece_s):
    j = pl.program_id(1)
    tb = h2_ref.shape[0]
    ec = u_ref.shape[0]
    rows_per_step = ec // N_KEYS
    n_sub = tb // LANES
    gdt = r2_s.dtype
    pack = SUBLANES * (4 // jnp.dtype(gdt).itemsize)
    mxu_cols, mxu_rows = _piece_shape(tb, ec)
    pieces = [(s0, e0) for s0 in range(0, tb, mxu_cols) for e0 in range(0, ec, mxu_rows)]
    a_p = piece_s[:len(pieces)]
    w_p = piece_s[len(pieces):2 * len(pieces)]
    acc_p = piece_s[2 * len(pieces):]

    @pl.when(j == 0)
    def _route():
        qt = lax.dot_general(wqt_ref[...], h2_ref[...], _NT, preferred_element_type=jnp.float32)
        qt_s[:, :tb] = _bf16(qt)
        for acc in acc_p:
            acc[...] = jnp.zeros_like(acc)

        def sub_body(t, carry):
            lanes = pl.ds(pl.multiple_of(t * LANES, LANES), LANES)
            cidx = cidx_ref[...]

            def route(exact):
                k = float(PEER_TOPK)
                tied_keys = jnp.zeros((1, LANES), jnp.float32)
                tied_grid = jnp.zeros((1, LANES), jnp.float32)
                for h in range(PEER_HEADS):
                    s = []
                    for p in range(2):
                        hp = 2 * h + p
                        s.append(jnp.dot(keys_ref[hp], qt_s[hp * N_KEYS:(hp + 1) * N_KEYS, lanes],
                                         preferred_element_type=jnp.float32))
                    code1, sv1 = _topk_ranks(s[0], exact)
                    code2, sv2 = _topk_ranks(s[1], exact)
                    e1 = jnp.exp(s[0] - sv1[0:1])
                    n1, a1, tg = _grid_stage(code1, e1, sv1, sv2, cidx, exact)
                    n1_s[h, :, lanes] = n1
                    a1_s[h, :, lanes] = a1
                    r2_s[h, :, lanes] = code2.astype(gdt)
                    b2_s[h, :, lanes] = jnp.exp(s[1] - sv2[0:1]).astype(gdt)
                    c1_s[h, :, lanes] = code1
                    e1_s[h, :, lanes] = e1
                    sv_s[h, 0:PEER_TOPK, lanes] = sv1
                    sv_s[h, PEER_TOPK:2 * PEER_TOPK, lanes] = sv2
                    tied_keys = (tied_keys + jnp.abs(_count(code1 > -jnp.inf) - k)
                                 + jnp.abs(_count(code2 > -jnp.inf) - k))
                    tied_grid = tied_grid + tg
                return jnp.max(tied_keys) > 0.0, jnp.max(tied_grid) > 0.0

            keys_tied, grid_tied = route(False)

            @pl.when(keys_tied)
            def _redo_all():
                route(True)

            @pl.when(jnp.logical_and(jnp.logical_not(keys_tied), grid_tied))
            def _redo_grid():
                for h in range(PEER_HEADS):
                    n1, a1, _ = _grid_stage(c1_s[h, :, lanes], e1_s[h, :, lanes],
                                            sv_s[h, 0:PEER_TOPK, lanes],
                                            sv_s[h, PEER_TOPK:2 * PEER_TOPK, lanes], cidx, True)
                    n1_s[h, :, lanes] = n1
                    a1_s[h, :, lanes] = a1
            return carry

        lax.fori_loop(0, n_sub, sub_body, 0)

    c0 = pl.multiple_of(j * rows_per_step, SUBLANES)

    def mm1(k):
        s0, e0 = pieces[k]
        a_p[k][:, :mxu_cols] = lax.dot_general(
            u_ref[e0:e0 + mxu_rows, :], h2_ref[s0:s0 + mxu_cols, :], _NT,
            preferred_element_type=jnp.float32)

    def gate_piece(k):
        s0, e0 = pieces[k]
        for t in range(mxu_cols // LANES):
            lanes = slice(s0 + t * LANES, s0 + (t + 1) * LANES)
            plane = slice(t * LANES, (t + 1) * LANES)
            for r in range(mxu_rows // N_KEYS):
                c = e0 // N_KEYS + r
                slab = pl.ds(c0 + (c // SUBLANES) * SUBLANES, SUBLANES)
                rr = c % SUBLANES
                gate = [jnp.zeros((pack, LANES), gdt) for _ in range(N_KEYS // pack)]
                for h in range(PEER_HEADS):
                    n1 = jnp.broadcast_to(n1_s[h, slab, lanes][rr:rr + 1], (pack, LANES)).astype(gdt)
                    a1 = jnp.broadcast_to(a1_s[h, slab, lanes][rr:rr + 1], (pack, LANES)).astype(gdt)
                    for kk in range(N_KEYS // pack):
                        i2 = slice(kk * pack, (kk + 1) * pack)
                        hit = r2_s[h, i2, lanes] >= n1
                        gate[kk] = gate[kk] + jnp.where(hit, b2_s[h, i2, lanes], 0.0).astype(gdt) * a1
                for kk in range(N_KEYS // pack):
                    rows = slice(r * N_KEYS + kk * pack, r * N_KEYS + (kk + 1) * pack)
                    a = a_p[k][rows, plane]
                    act = a * (1.0 + lax.erf(a))
                    w_p[k][rows, plane] = (act.astype(gdt) * gate[kk]).astype(w_p[k].dtype)

    def mm2(k):
        s0, e0 = pieces[k]
        acc_p[s0 // mxu_cols][...] += jnp.dot(vt_ref[:, e0:e0 + mxu_rows], w_p[k][:, :mxu_cols],
                                              preferred_element_type=jnp.float32)

    for k in range(len(pieces) + 2):
        if k < len(pieces):
            mm1(k)
        if 1 <= k <= len(pieces):
            gate_piece(k - 1)
        if k >= 2:
            mm2(k - 2)

    @pl.when(j == pl.num_programs(1) - 1)
    def _finish():
        gt2 = gt2_ref[0]
        for ci, acc in enumerate(acc_p):
            tok = slice(ci * mxu_cols, (ci + 1) * mxu_cols)
            g = gt2 if gt2.shape[0] == 1 else gt2[tok]
            y_ref[tok, :] = x1_ref[tok, :] + g * _rms(acc[...].T, g_post_ref[...])


def _piece_shape(tb, ec):
    return min(tb, 2 * LANES), ec // 2


def _odd_lane_tiles(n):
    tiles = -(-n // LANES)
    return (tiles + 1 - tiles % 2) * LANES


def _peer(h2, x1, mod, wqt, keys, cidx, g_post, u_bf, vt_bf, *, tb, blocks_per_mod, ec):
    t, d = h2.shape
    ne = u_bf.shape[0]
    mod_rows = mod.shape[1]
    n_gt2 = 5
    tbp = _odd_lane_tiles(tb)
    mxu_cols, mxu_rows = _piece_shape(tb, ec)
    n_col = tb // mxu_cols
    n_pieces = n_col * (ec // mxu_rows)
    colp = _odd_lane_tiles(mxu_cols)
    return pl.pallas_call(
        _peer_kernel,
        grid=(t // tb, ne // ec),
        in_specs=[
            pl.BlockSpec((tb, d), lambda i, j: (i, 0)),
            pl.BlockSpec((tb, d), lambda i, j: (i, 0)),
            pl.BlockSpec((1, mod_rows, d), lambda i, j: (i // blocks_per_mod, 0, n_gt2)),
            _const_spec(wqt.shape), _const_spec(keys.shape), _const_spec(cidx.shape),
            _const_spec(g_post.shape),
            pl.BlockSpec((ec, d), lambda i, j: (j, 0)),
            pl.BlockSpec((d, ec), lambda i, j: (0, j)),
        ],
        out_specs=pl.BlockSpec((tb, d), lambda i, j: (i, 0)),
        out_shape=jax.ShapeDtypeStruct((t, d), jnp.float32),
        scratch_shapes=[
            pltpu.VMEM((wqt.shape[0], tbp), MXU_DTYPE),
            pltpu.VMEM((PEER_HEADS, N_KEYS, tbp), jnp.float32),
            pltpu.VMEM((PEER_HEADS, N_KEYS, tbp), jnp.float32),
            pltpu.VMEM((PEER_HEADS, N_KEYS, tbp), GATE_DTYPE),
            pltpu.VMEM((PEER_HEADS, N_KEYS, tbp), GATE_DTYPE),
            pltpu.VMEM((PEER_HEADS, N_KEYS, tbp), jnp.float32),
            pltpu.VMEM((PEER_HEADS, N_KEYS, tbp), jnp.float32),
            pltpu.VMEM((PEER_HEADS, 2 * PEER_TOPK, tbp), jnp.float32),
        ] + [pltpu.VMEM((mxu_rows, colp), jnp.float32) for _ in range(n_pieces)]
          + [pltpu.VMEM((mxu_rows, colp), MXU_DTYPE) for _ in range(n_pieces)]
          + [pltpu.VMEM((d, mxu_cols), jnp.float32) for _ in range(n_col)],
        compiler_params=pltpu.CompilerParams(
            dimension_semantics=("arbitrary", "arbitrary"), vmem_limit_bytes=VMEM_LIMIT_BYTES),
        name="peer",
    )(h2, x1, mod, wqt, keys, cidx, g_post, u_bf, vt_bf)


def _tile_sizes(l):
    return min(512, l), min(512, l), 1024


def kernel(x_prompt, x_sample, state_conv, c_prompt, c_sample, w_ada, b_ada, g_pre_mix, w_in, conv_w,
           sgu_norm_g, w_s, b_s, w_out, g_post_mix, g_pre_ffn, w_q, sub_keys, expert_u, expert_v,
           g_post_ffn):
    depth = w_ada.shape[0]
    assert depth == 1, "single-layer step"
    b, l, d = x_prompt.shape
    nb = x_sample.shape[0]
    assert x_sample.shape[1] == 1 and l % CHUNK == 0
    cdim = conv_w.shape[1]
    sdim = sgu_norm_g.shape[1]
    gdim = sdim // SGU_GROUPS
    assert gdim == LANES and sub_keys.shape[3] == N_KEYS and sub_keys.shape[1] == PEER_HEADS
    ne = expert_u.shape[1]
    assert ne == N_KEYS * N_KEYS

    row = lambda a: a.reshape(1, -1)
    w_ada_b = _bf16(w_ada[0])
    w_in_b = _bf16(w_in[0])
    w_out_b = _bf16(w_out[0])
    wqt_b = _bf16(w_q[0].T)
    keys_b = _bf16(sub_keys[0].reshape(2 * PEER_HEADS, N_KEYS, -1))
    cw = conv_w[0].T
    bias = jnp.repeat(b_s[0].T, gdim, axis=1)
    w00 = jnp.repeat(w_s[0][:, 0, 0], gdim).reshape(1, sdim)
    b0 = bias[0:1]
    cidx = jnp.asarray(_candidate_cells())

    c_all = jnp.concatenate([c_prompt, c_sample], axis=0)
    mod = _ada(c_all, w_ada_b, row(b_ada[0]))
    mod_p = mod[:b].reshape(b, 1, -1)
    mod_s = mod[b:]

    tm, tb, ec = _tile_sizes(l)
    x1p, h2p, conv_p, u_b, vt_b = _mixer_prompt(
        x_prompt, mod_p, row(g_pre_mix[0]), w_in_b, cw, row(sgu_norm_g[0]), w_s[0], bias, w_out_b,
        row(g_post_mix[0]), row(g_pre_ffn[0]), expert_u[0], expert_v[0], tm=tm)
    x1s, h2s, conv_s, vn_s = _mixer_sample(
        x_sample.reshape(nb, d), mod_s, state_conv[0].reshape(nb, -1), row(g_pre_mix[0]), w_in_b, cw,
        row(sgu_norm_g[0]), w00, b0, w_out_b, row(g_post_mix[0]), row(g_pre_ffn[0]))

    yp = _peer(h2p.reshape(b * l, d), x1p.reshape(b * l, d), mod_p, wqt_b, keys_b, cidx,
               row(g_post_ffn[0]), u_b, vt_b, tb=tb, blocks_per_mod=l // tb, ec=ec)
    ys = _peer(h2s, x1s, mod_s.reshape(1, nb, -1), wqt_b, keys_b, cidx,
               row(g_post_ffn[0]), u_b, vt_b, tb=nb, blocks_per_mod=1, ec=ec)

    return (yp.reshape(b, l, d), ys.reshape(nb, 1, d),
            conv_p.reshape(1, b, CONV_W - 1, cdim), conv_s.reshape(1, nb, CONV_W - 1, cdim),
            vn_s.reshape(1, nb, 1, sdim))
```

```python
import functools

import numpy as np
import jax
import jax.numpy as jnp
from jax import lax
from jax.experimental import pallas as pl
from jax.experimental.pallas import tpu as pltpu

EPS = 1e-6
CONV_W = 3
SGU_GROUPS = 4
CHUNK = 128
PEER_HEADS = 8
N_KEYS = 128
PEER_TOPK = 16
LANES = 128
SUBLANES = 8
RANK_MARK = 2.0 ** 20
VMEM_LIMIT_BYTES = 56 * 1024 * 1024

_NT = (((1,), (1,)), ((), ()))


def _rms(x, g):
    return x * lax.rsqrt(jnp.mean(x * x, axis=-1, keepdims=True) + EPS) * g


MXU_DTYPE = jnp.bfloat16
GATE_DTYPE = jnp.float32


def _bf16(x):
    return x.astype(MXU_DTYPE)


def _ada_kernel(c_ref, w_ref, b_ref, o_ref):
    c = c_ref[...]
    s = c * (1.0 / (1.0 + jnp.exp(-c)))
    o_ref[...] = jnp.dot(_bf16(s), w_ref[...], preferred_element_type=jnp.float32) + b_ref[...]


def _ada(c, w_ada, b_ada):
    n, d = c.shape
    e = w_ada.shape[1]
    tn = d
    return pl.pallas_call(
        _ada_kernel,
        grid=(e // tn,),
        in_specs=[
            pl.BlockSpec((n, d), lambda j: (0, 0)),
            pl.BlockSpec((d, tn), lambda j: (0, j)),
            pl.BlockSpec((1, tn), lambda j: (0, j)),
        ],
        out_specs=pl.BlockSpec((n, tn), lambda j: (0, j)),
        out_shape=jax.ShapeDtypeStruct((n, e), jnp.float32),
        name="ada",
    )(c, w_ada, b_ada)


def _group_norm(v, gs, gdim):
    outs = []
    for g in range(SGU_GROUPS):
        vg = v[:, g * gdim:(g + 1) * gdim]
        mu = jnp.mean(vg, axis=-1, keepdims=True)
        d = vg - mu
        yn = d * lax.rsqrt(jnp.mean(d * d, axis=-1, keepdims=True) + EPS)
        outs.append(yn * gs[:, g * gdim:(g + 1) * gdim])
    return jnp.concatenate(outs, axis=-1)


def _mixer_tail(x, yc, ys, mod, d, w_out_ref, g_post_ref, g_ffn_ref, x1_ref, h2_ref):
    gt1 = mod[:, 2 * d:3 * d]
    sh2 = mod[:, 3 * d:4 * d]
    sc2 = mod[:, 4 * d:5 * d]
    mix_in = _bf16(jnp.concatenate([yc, ys], axis=-1))
    mix = jnp.dot(mix_in, w_out_ref[...], preferred_element_type=jnp.float32)
    x1 = x + gt1 * _rms(mix, g_post_ref[...])
    h2 = _rms(x1, g_ffn_ref[...]) * (1.0 + sc2) + sh2
    return x1, h2


def _mixer_prompt_kernel(x_ref, mod_ref, g_pre_ref, w_in_ref, cw_ref, gs_ref, ws_ref, bias_ref,
                         w_out_ref, g_post_ref, g_ffn_ref, eu_ref, ev_ref,
                         x1_ref, h2_ref, conv_ref, ub_ref, vtb_ref, carry_ref, *, d, cdim, sdim):
    tm = x_ref.shape[1]
    gdim = sdim // SGU_GROUPS

    @pl.when(pl.program_id(1) == 0)
    def _():
        carry_ref[...] = jnp.zeros_like(carry_ref)

    x = x_ref[0]
    mod = mod_ref[0]
    sh1 = mod[:, 0:d]
    sc1 = mod[:, d:2 * d]
    h = _rms(x, g_pre_ref[...]) * (1.0 + sc1) + sh1
    p = jnp.dot(_bf16(h), w_in_ref[...], preferred_element_type=jnp.float32)
    b_gate = p[:, 0:cdim]
    z = p[:, cdim:2 * cdim] * p[:, 2 * cdim:3 * cdim]
    u = p[:, 3 * cdim:3 * cdim + sdim]
    v = p[:, 3 * cdim + sdim:]

    zext = jnp.concatenate([carry_ref[...], z], axis=0)
    z1 = pltpu.roll(zext, 1, 0)[SUBLANES:]
    z2 = pltpu.roll(zext, 2, 0)[SUBLANES:]
    cw = cw_ref[...]
    y = z2 * cw[0:1] + z1 * cw[1:2] + z * cw[2:3]
    yc = b_gate * y
    carry_ref[...] = z[tm - SUBLANES:]
    conv_ref[0] = z[tm - (CONV_W - 1):]

    vn = _group_norm(v, gs_ref[...], gdim)
    vnb = _bf16(vn)
    row = lax.broadcasted_iota(jnp.int32, (CHUNK, CHUNK), 0)
    col = lax.broadcasted_iota(jnp.int32, (CHUNK, CHUNK), 1)
    bias = bias_ref[...]
    chunks = []
    for ch in range(tm // CHUNK):
        groups = []
        for g in range(SGU_GROUPS):
            wt = _bf16(jnp.where(row >= col, ws_ref[g], 0.0))
            groups.append(jnp.dot(wt, vnb[ch * CHUNK:(ch + 1) * CHUNK, g * gdim:(g + 1) * gdim],
                                  preferred_element_type=jnp.float32))
        chunks.append(jnp.concatenate(groups, axis=-1) + bias)
    mixed = jnp.concatenate(chunks, axis=0)
    ys = u * mixed

    x1, h2 = _mixer_tail(x, yc, ys, mod, d, w_out_ref, g_post_ref, g_ffn_ref, x1_ref, h2_ref)
    x1_ref[0] = x1
    h2_ref[0] = _bf16(h2)

    ub_ref[...] = _bf16(eu_ref[...])
    vtb_ref[...] = _bf16(ev_ref[...].T)


def _mixer_sample_kernel(x_ref, mod_ref, st_ref, g_pre_ref, w_in_ref, cw_ref, gs_ref, w00_ref, b0_ref,
                         w_out_ref, g_post_ref, g_ffn_ref,
                         x1_ref, h2_ref, conv_ref, vn_ref, *, d, cdim, sdim):
    gdim = sdim // SGU_GROUPS
    x = x_ref[...]
    mod = mod_ref[...]
    sh1 = mod[:, 0:d]
    sc1 = mod[:, d:2 * d]
    h = _rms(x, g_pre_ref[...]) * (1.0 + sc1) + sh1
    p = jnp.dot(_bf16(h), w_in_ref[...], preferred_element_type=jnp.float32)
    b_gate = p[:, 0:cdim]
    z = p[:, cdim:2 * cdim] * p[:, 2 * cdim:3 * cdim]
    u = p[:, 3 * cdim:3 * cdim + sdim]
    v = p[:, 3 * cdim + sdim:]

    st = st_ref[...]
    z2 = st[:, 0:cdim]
    z1 = st[:, cdim:]
    cw = cw_ref[...]
    y = z2 * cw[0:1] + z1 * cw[1:2] + z * cw[2:3]
    yc = b_gate * y
    conv_ref[...] = jnp.concatenate([z1, z], axis=-1)

    vn = _group_norm(v, gs_ref[...], gdim)
    vn_ref[...] = vn
    mixed = vn * w00_ref[...] + b0_ref[...]
    ys = u * mixed

    x1, h2 = _mixer_tail(x, yc, ys, mod, d, w_out_ref, g_post_ref, g_ffn_ref, x1_ref, h2_ref)
    x1_ref[...] = x1
    h2_ref[...] = _bf16(h2)


def _const_spec(shape):
    zeros = (0,) * len(shape)
    return pl.BlockSpec(shape, lambda *_: zeros)


def _mixer_prompt(x, mod, g_pre, w_in, cw, gs, ws, bias, w_out, g_post, g_ffn, eu, ev, *, tm):
    b, l, d = x.shape
    cdim = cw.shape[1]
    sdim = gs.shape[1]
    ne = eu.shape[0]
    steps_per_seq = l // tm
    rows = ne // (b * steps_per_seq)
    assert rows * b * steps_per_seq == ne and rows % LANES == 0
    kern = functools.partial(_mixer_prompt_kernel, d=d, cdim=cdim, sdim=sdim)
    tok = lambda i, s: (i, s, 0)
    return pl.pallas_call(
        kern,
        grid=(b, l // tm),
        in_specs=[
            pl.BlockSpec((1, tm, d), tok),
            pl.BlockSpec((1, 1, mod.shape[2]), lambda i, s: (i, 0, 0)),
            _const_spec(g_pre.shape), _const_spec(w_in.shape), _const_spec(cw.shape),
            _const_spec(gs.shape), _const_spec(ws.shape), _const_spec(bias.shape),
            _const_spec(w_out.shape), _const_spec(g_post.shape), _const_spec(g_ffn.shape),
            pl.BlockSpec((rows, d), lambda i, s: (i * steps_per_seq + s, 0)),
            pl.BlockSpec((rows, d), lambda i, s: (i * steps_per_seq + s, 0)),
        ],
        out_specs=[
            pl.BlockSpec((1, tm, d), tok),
            pl.BlockSpec((1, tm, d), tok),
            pl.BlockSpec((1, CONV_W - 1, cdim), lambda i, s: (i, 0, 0)),
            pl.BlockSpec((rows, d), lambda i, s: (i * steps_per_seq + s, 0)),
            pl.BlockSpec((d, rows), lambda i, s: (0, i * steps_per_seq + s)),
        ],
        out_shape=[
            jax.ShapeDtypeStruct((b, l, d), jnp.float32),
            jax.ShapeDtypeStruct((b, l, d), MXU_DTYPE),
            jax.ShapeDtypeStruct((b, CONV_W - 1, cdim), jnp.float32),
            jax.ShapeDtypeStruct((ne, d), MXU_DTYPE),
            jax.ShapeDtypeStruct((d, ne), MXU_DTYPE),
        ],
        scratch_shapes=[pltpu.VMEM((SUBLANES, cdim), jnp.float32)],
        compiler_params=pltpu.CompilerParams(
            dimension_semantics=("arbitrary", "arbitrary"), vmem_limit_bytes=VMEM_LIMIT_BYTES),
        name="mixer_prompt",
    )(x, mod, g_pre, w_in, cw, gs, ws, bias, w_out, g_post, g_ffn, eu, ev)


def _mixer_sample(x, mod, st, g_pre, w_in, cw, gs, w00, b0, w_out, g_post, g_ffn):
    n, d = x.shape
    cdim = cw.shape[1]
    sdim = gs.shape[1]
    kern = functools.partial(_mixer_sample_kernel, d=d, cdim=cdim, sdim=sdim)
    args = (x, mod, st, g_pre, w_in, cw, gs, w00, b0, w_out, g_post, g_ffn)
    return pl.pallas_call(
        kern,
        grid=(1,),
        in_specs=[_const_spec(a.shape) for a in args],
        out_specs=[_const_spec((n, d)), _const_spec((n, d)),
                   _const_spec((n, (CONV_W - 1) * cdim)), _const_spec((n, sdim))],
        out_shape=[
            jax.ShapeDtypeStruct((n, d), jnp.float32),
            jax.ShapeDtypeStruct((n, d), MXU_DTYPE),
            jax.ShapeDtypeStruct((n, (CONV_W - 1) * cdim), jnp.float32),
            jax.ShapeDtypeStruct((n, sdim), jnp.float32),
        ],
        compiler_params=pltpu.CompilerParams(vmem_limit_bytes=VMEM_LIMIT_BYTES),
        name="mixer_sample",
    )(*args)


def _candidate_cells():
    k = PEER_TOPK
    idx = np.full((10 * SUBLANES,), 1e9, np.float32)
    idx[0:k] = np.arange(k)
    idx[k:k + 8] = k + np.arange(8)
    for a in range(2, 8):
        nb = k // (a + 1)
        base = 24 + 8 * (a - 2)
        idx[base:base + nb] = a * k + np.arange(nb)
    idx[72:80] = (8 + np.arange(8)) * k
    return np.broadcast_to(idx[:, None], (idx.shape[0], LANES)).copy()


def _rank_code(a):
    return -(a + 1) * RANK_MARK


def _topk_ranks(s, exact):
    iota16 = lax.broadcasted_iota(jnp.int32, (PEER_TOPK, s.shape[1]), 0)
    sv = jnp.zeros((PEER_TOPK, s.shape[1]), jnp.float32)
    work = s
    if exact:
        iota = lax.broadcasted_iota(jnp.int32, s.shape, 0).astype(jnp.float32)
        code = jnp.full(s.shape, -jnp.inf, jnp.float32)
        for a in range(PEER_TOPK):
            mx = jnp.max(work, axis=0, keepdims=True)
            idx = jnp.min(jnp.where(work == mx, iota, float(N_KEYS)), axis=0, keepdims=True)
            hit = iota == idx
            code = jnp.where(hit, _rank_code(a), code)
            work = jnp.where(hit, -jnp.inf, work)
            sv = jnp.where(iota16 == a, mx, sv)
        return code, sv
    for a in range(PEER_TOPK):
        mx = jnp.max(work, axis=0, keepdims=True)
        work = jnp.where(work == mx, _rank_code(a), work)
        sv = jnp.where(iota16 == a, mx, sv)
    return jnp.where(work <= -RANK_MARK, work, -jnp.inf), sv


def _count(mask):
    return jnp.sum(jnp.where(mask, 1.0, 0.0), axis=0, keepdims=True)


def _grid_stage(code1, e1, sv1, sv2, cidx, exact):
    pieces = [sv2 + sv1[0:1], sv2[0:8] + sv1[1:2]]
    for a in range(2, 8):
        pieces.append(sv2[0:8] + sv1[a:a + 1])
    pieces.append(sv1[8:16] + sv2[0:1])
    cand = jnp.concatenate(pieces, axis=0)
    cand = jnp.where(cidx < 1e8, cand, -jnp.inf)
    first = cidx == 0.0
    sel = jnp.where(first, 1.0, 0.0)
    work = jnp.where(first, -jnp.inf, cand)
    for _ in range(PEER_TOPK - 1):
        mx = jnp.max(work, axis=0, keepdims=True)
        if exact:
            idx = jnp.min(jnp.where(work == mx, cidx, 1e9), axis=0, keepdims=True)
            hit = cidx == idx
        else:
            hit = work == mx
        sel = jnp.where(hit, 1.0, sel)
        work = jnp.where(hit, -jnp.inf, work)
    m = cand[0:1]
    z = jnp.sum(sel * jnp.exp(cand - m), axis=0, keepdims=True)
    thr = sel * -RANK_MARK
    n1 = jnp.zeros(code1.shape, jnp.float32)
    n1 = jnp.where(code1 == _rank_code(0), jnp.sum(thr[0:16], axis=0, keepdims=True), n1)
    for a in range(1, 8):
        base = 16 + 8 * (a - 1)
        n1 = jnp.where(code1 == _rank_code(a), jnp.sum(thr[base:base + 8], axis=0, keepdims=True), n1)
    for a in range(8, 16):
        n1 = jnp.where(code1 == _rank_code(a), thr[72 + a - 8:73 + a - 8], n1)
    a1 = 0.5 * e1 / z
    tied = jnp.abs(jnp.sum(sel, axis=0, keepdims=True) - float(PEER_TOPK))
    return n1, a1, tied


def _peer_kernel(h2_ref, x1_ref, gt2_ref, wqt_ref, keys_ref, cidx_ref, g_post_ref, u_ref, vt_ref,
                 y_ref,
                 qt_s, n1_s, a1_s, r2_s, b2_s, c1_s, e1_s, sv_s, *piece_s):
    j = pl.program_id(1)
    tb = h2_ref.shape[0]
    ec = u_ref.shape[0]
    rows_per_step = ec // N_KEYS
    n_sub = tb // LANES
    gdt = r2_s.dtype
    pack = SUBLANES * (4 // jnp.dtype(gdt).itemsize)
    mxu_cols, mxu_rows = _piece_shape(tb, ec)
    pieces = [(s0, e0) for s0 in range(0, tb, mxu_cols) for e0 in range(0, ec, mxu_rows)]
    a_p = piece_s[:len(pieces)]
    w_p = piece_s[len(pieces):2 * len(pieces)]
    acc_p = piece_s[2 * len(pieces):]

    @pl.when(j == 0)
    def _route():
        qt = lax.dot_general(wqt_ref[...], h2_ref[...], _NT, preferred_element_type=jnp.float32)
        qt_s[:, :tb] = _bf16(qt)
        for acc in acc_p:
            acc[...] = jnp.zeros_like(acc)

        def sub_body(t, carry):
            lanes = pl.ds(pl.multiple_of(t * LANES, LANES), LANES)
            cidx = cidx_ref[...]

            def route(exact):
                k = float(PEER_TOPK)
                tied_keys = jnp.zeros((1, LANES), jnp.float32)
                tied_grid = jnp.zeros((1, LANES), jnp.float32)
                for h in range(PEER_HEADS):
                    s = []
                    for p in range(2):
                        hp = 2 * h + p
                        s.append(jnp.dot(keys_ref[hp], qt_s[hp * N_KEYS:(hp + 1) * N_KEYS, lanes],
                                         preferred_element_type=jnp.float32))
                    code1, sv1 = _topk_ranks(s[0], exact)
                    code2, sv2 = _topk_ranks(s[1], exact)
                    e1 = jnp.exp(s[0] - sv1[0:1])
                    n1, a1, tg = _grid_stage(code1, e1, sv1, sv2, cidx, exact)
                    n1_s[h, :, lanes] = n1
                    a1_s[h, :, lanes] = a1
                    r2_s[h, :, lanes] = code2.astype(gdt)
                    b2_s[h, :, lanes] = jnp.exp(s[1] - sv2[0:1]).astype(gdt)
                    c1_s[h, :, lanes] = code1
                    e1_s[h, :, lanes] = e1
                    sv_s[h, 0:PEER_TOPK, lanes] = sv1
                    sv_s[h, PEER_TOPK:2 * PEER_TOPK, lanes] = sv2
                    tied_keys = (tied_keys + jnp.abs(_count(code1 > -jnp.inf) - k)
                                 + jnp.abs(_count(code2 > -jnp.inf) - k))
                    tied_grid = tied_grid + tg
                return jnp.max(tied_keys) > 0.0, jnp.max(tied_grid) > 0.0

            keys_tied, grid_tied = route(False)

            @pl.when(keys_tied)
            def _redo_all():
                route(True)

            @pl.when(jnp.logical_and(jnp.logical_not(keys_tied), grid_tied))
            def _redo_grid():
                for h in range(PEER_HEADS):
                    n1, a1, _ = _grid_stage(c1_s[h, :, lanes], e1_s[h, :, lanes],
                                            sv_s[h, 0:PEER_TOPK, lanes],
                                            sv_s[h, PEER_TOPK:2 * PEER_TOPK, lanes], cidx, True)
                    n1_s[h, :, lanes] = n1
                    a1_s[h, :, lanes] = a1
            return carry

        lax.fori_loop(0, n_sub, sub_body, 0)

    c0 = pl.multiple_of(j * rows_per_step, SUBLANES)

    def mm1(k):
        s0, e0 = pieces[k]
        a_p[k][:, :mxu_cols] = lax.dot_general(
            u_ref[e0:e0 + mxu_rows, :], h2_ref[s0:s0 + mxu_cols, :], _NT,
            preferred_element_type=jnp.float32)

    def gate_piece(k):
        s0, e0 = pieces[k]
        for t in range(mxu_cols // LANES):
            lanes = slice(s0 + t * LANES, s0 + (t + 1) * LANES)
            plane = slice(t * LANES, (t + 1) * LANES)
            for r in range(mxu_rows // N_KEYS):
                c = e0 // N_KEYS + r
                slab = pl.ds(c0 + (c // SUBLANES) * SUBLANES, SUBLANES)
                rr = c % SUBLANES
                gate = [jnp.zeros((pack, LANES), gdt) for _ in range(N_KEYS // pack)]
                for h in range(PEER_HEADS):
                    n1 = jnp.broadcast_to(n1_s[h, slab, lanes][rr:rr + 1], (pack, LANES)).astype(gdt)
                    a1 = jnp.broadcast_to(a1_s[h, slab, lanes][rr:rr + 1], (pack, LANES)).astype(gdt)
                    for kk in range(N_KEYS // pack):
                        i2 = slice(kk * pack, (kk + 1) * pack)
                        hit = r2_s[h, i2, lanes] >= n1
                        gate[kk] = gate[kk] + jnp.where(hit, b2_s[h, i2, lanes], 0.0).astype(gdt) * a1
                for kk in range(N_KEYS // pack):
                    rows = slice(r * N_KEYS + kk * pack, r * N_KEYS + (kk + 1) * pack)
                    a = a_p[k][rows, plane]
                    act = a * (1.0 + lax.erf(a * np.float32(np.sqrt(0.5))))
                    w_p[k][rows, plane] = (act.astype(gdt) * gate[kk]).astype(w_p[k].dtype)

    def mm2(k):
        s0, e0 = pieces[k]
        acc_p[s0 // mxu_cols][...] += jnp.dot(vt_ref[:, e0:e0 + mxu_rows], w_p[k][:, :mxu_cols],
                                              preferred_element_type=jnp.float32)

    for k in range(len(pieces) + 2):
        if k < len(pieces):
            mm1(k)
        if 1 <= k <= len(pieces):
            gate_piece(k - 1)
        if k >= 2:
            mm2(k - 2)

    @pl.when(j == pl.num_programs(1) - 1)
    def _finish():
        gt2 = gt2_ref[0]
        for ci, acc in enumerate(acc_p):
            tok = slice(ci * mxu_cols, (ci + 1) * mxu_cols)
            g = gt2 if gt2.shape[0] == 1 else gt2[tok]
            y_ref[tok, :] = x1_ref[tok, :] + g * _rms(acc[...].T, g_post_ref[...])


def _piece_shape(tb, ec):
    return min(tb, 2 * LANES), ec // 2


def _odd_lane_tiles(n):
    tiles = -(-n // LANES)
    return (tiles + 1 - tiles % 2) * LANES


def _peer(h2, x1, mod, wqt, keys, cidx, g_post, u_bf, vt_bf, *, tb, blocks_per_mod, ec):
    t, d = h2.shape
    ne = u_bf.shape[0]
    mod_rows = mod.shape[1]
    n_gt2 = 5
    tbp = _odd_lane_tiles(tb)
    mxu_cols, mxu_rows = _piece_shape(tb, ec)
    n_col = tb // mxu_cols
    n_pieces = n_col * (ec // mxu_rows)
    colp = _odd_lane_tiles(mxu_cols)
    return pl.pallas_call(
        _peer_kernel,
        grid=(t // tb, ne // ec),
        in_specs=[
            pl.BlockSpec((tb, d), lambda i, j: (i, 0)),
            pl.BlockSpec((tb, d), lambda i, j: (i, 0)),
            pl.BlockSpec((1, mod_rows, d), lambda i, j: (i // blocks_per_mod, 0, n_gt2)),
            _const_spec(wqt.shape), _const_spec(keys.shape), _const_spec(cidx.shape),
            _const_spec(g_post.shape),
            pl.BlockSpec((ec, d), lambda i, j: (j, 0)),
            pl.BlockSpec((d, ec), lambda i, j: (0, j)),
        ],
        out_specs=pl.BlockSpec((tb, d), lambda i, j: (i, 0)),
        out_shape=jax.ShapeDtypeStruct((t, d), jnp.float32),
        scratch_shapes=[
            pltpu.VMEM((wqt.shape[0], tbp), MXU_DTYPE),
            pltpu.VMEM((PEER_HEADS, N_KEYS, tbp), jnp.float32),
            pltpu.VMEM((PEER_HEADS, N_KEYS, tbp), jnp.float32),
            pltpu.VMEM((PEER_HEADS, N_KEYS, tbp), GATE_DTYPE),
            pltpu.VMEM((PEER_HEADS, N_KEYS, tbp), GATE_DTYPE),
            pltpu.VMEM((PEER_HEADS, N_KEYS, tbp), jnp.float32),
            pltpu.VMEM((PEER_HEADS, N_KEYS, tbp), jnp.float32),
            pltpu.VMEM((PEER_HEADS, 2 * PEER_TOPK, tbp), jnp.float32),
        ] + [pltpu.VMEM((mxu_rows, colp), jnp.float32) for _ in range(n_pieces)]
          + [pltpu.VMEM((mxu_rows, colp), MXU_DTYPE) for _ in range(n_pieces)]
          + [pltpu.VMEM((d, mxu_cols), jnp.float32) for _ in range(n_col)],
        compiler_params=pltpu.CompilerParams(
            dimension_semantics=("arbitrary", "arbitrary"), vmem_limit_bytes=VMEM_LIMIT_BYTES),
        name="peer",
    )(h2, x1, mod, wqt, keys, cidx, g_post, u_bf, vt_bf)


def _tile_sizes(l):
    return min(256, l), min(512, l), 1024


def kernel(x_prompt, x_sample, state_conv, c_prompt, c_sample, w_ada, b_ada, g_pre_mix, w_in, conv_w,
           sgu_norm_g, w_s, b_s, w_out, g_post_mix, g_pre_ffn, w_q, sub_keys, expert_u, expert_v,
           g_post_ffn):
    depth = w_ada.shape[0]
    assert depth == 1, "single-layer step"
    b, l, d = x_prompt.shape
    nb = x_sample.shape[0]
    assert x_sample.shape[1] == 1 and l % CHUNK == 0
    cdim = conv_w.shape[1]
    sdim = sgu_norm_g.shape[1]
    gdim = sdim // SGU_GROUPS
    assert gdim == LANES and sub_keys.shape[3] == N_KEYS and sub_keys.shape[1] == PEER_HEADS
    ne = expert_u.shape[1]
    assert ne == N_KEYS * N_KEYS

    row = lambda a: a.reshape(1, -1)
    w_ada_b = _bf16(w_ada[0])
    w_in_b = _bf16(w_in[0])
    w_out_b = _bf16(w_out[0])
    wqt_b = _bf16(w_q[0].T)
    keys_b = _bf16(sub_keys[0].reshape(2 * PEER_HEADS, N_KEYS, -1))
    cw = conv_w[0].T
    bias = jnp.repeat(b_s[0].T, gdim, axis=1)
    w00 = jnp.repeat(w_s[0][:, 0, 0], gdim).reshape(1, sdim)
    b0 = bias[0:1]
    cidx = jnp.asarray(_candidate_cells())

    c_all = jnp.concatenate([c_prompt, c_sample], axis=0)
    mod = _ada(c_all, w_ada_b, row(b_ada[0]))
    mod_p = mod[:b].reshape(b, 1, -1)
    mod_s = mod[b:]

    tm, tb, ec = _tile_sizes(l)
    x1p, h2p, conv_p, u_b, vt_b = _mixer_prompt(
        x_prompt, mod_p, row(g_pre_mix[0]), w_in_b, cw, row(sgu_norm_g[0]), w_s[0], bias, w_out_b,
        row(g_post_mix[0]), row(g_pre_ffn[0]), expert_u[0], expert_v[0], tm=tm)
    x1s, h2s, conv_s, vn_s = _mixer_sample(
        x_sample.reshape(nb, d), mod_s, state_conv[0].reshape(nb, -1), row(g_pre_mix[0]), w_in_b, cw,
        row(sgu_norm_g[0]), w00, b0, w_out_b, row(g_post_mix[0]), row(g_pre_ffn[0]))

    yp = _peer(h2p.reshape(b * l, d), x1p.reshape(b * l, d), mod_p, wqt_b, keys_b, cidx,
               row(g_post_ffn[0]), u_b, vt_b, tb=tb, blocks_per_mod=l // tb, ec=ec)
    ys = _peer(h2s, x1s, mod_s.reshape(1, nb, -1), wqt_b, keys_b, cidx,
               row(g_post_ffn[0]), u_b, vt_b, tb=nb, blocks_per_mod=1, ec=ec)

    return (yp.reshape(b, l, d), ys.reshape(nb, 1, d),
            conv_p.reshape(1, b, CONV_W - 1, cdim), conv_s.reshape(1, nb, CONV_W - 1, cdim),
            vn_s.reshape(1, nb, 1, sdim))
```
